```python
import math
import jax
import jax.numpy as jnp
from jax import lax
import numpy as np

D_MODEL = 1024
BATCH = 8
SEQ = 4096
DEPTH = 2

N_META = 16
CHUNK = 128
PAD = CHUNK - N_META
WINDOW = 128
D_MIX = D_MODEL
CONV_K = 4
EPS = 1e-6
NEG = -1e30

DN_H = 4
DN_DK = 64
DN_DV = 64
DF_H = 4
DF_DQK = 32
DF_DV = 2 * DF_DQK
SW_HQ = 4
SW_HKV = 2
SW_REP = SW_HQ // SW_HKV
SW_D = 64
SSD_H = 4
SSD_P = 64
SSD_N = 128
SSD_G = 2
D_SSM = SSD_H * SSD_P

D_FF = 2816
N_EXP = 8
TOP_K = 2
D_FF_E = 3584
MOE_BLOCK = 128
N_DENSE = (DEPTH + 1) // 2
N_MOE = DEPTH // 2

COL_WIDTHS = (DN_H * DN_DK, DN_H * DN_DK, DN_H * DN_DV, DN_H * DN_DV, DN_H, DN_H,
              2 * DF_H * DF_DQK, 2 * DF_H * DF_DQK, DF_H * DF_DV,
              SW_HQ * SW_D, SW_HKV * SW_D, SW_HKV * SW_D,
              D_SSM, D_SSM + 2 * SSD_G * SSD_N, SSD_H)
N_COLS = sum(COL_WIDTHS)

kernel_name = "hymba_parallel_hybrid_trunk"


def rmsnorm(x, w):
    xf = x.astype(jnp.float32)
    y = xf * lax.rsqrt(jnp.mean(xf * xf, axis=-1, keepdims=True) + EPS)
    return (y * w.astype(jnp.float32)).astype(x.dtype)


def l2norm(x):
    return x * lax.rsqrt(jnp.sum(x * x, axis=-1, keepdims=True) + EPS)


def causal_conv(x, w):
    return lax.conv_general_dilated(x, w[:, None, :], window_strides=(1,),
                                    padding=[(w.shape[0] - 1, 0)],
                                    dimension_numbers=("NWC", "WIO", "NWC"),
                                    feature_group_count=x.shape[-1])


def front_pad(t):
    return jnp.pad(t, [(0, 0), (PAD, 0)] + [(0, 0)] * (t.ndim - 2))


def to_chunks(t):
    t = front_pad(t)
    bsz, lp = t.shape[:2]
    t = t.reshape((bsz, lp // CHUNK, CHUNK) + t.shape[2:])
    return jnp.moveaxis(t, 2, 3)


def from_chunks(t):
    t = jnp.moveaxis(t, 3, 2)
    bsz, nc = t.shape[:2]
    return t.reshape((bsz, nc * CHUNK) + t.shape[3:])[:, PAD:]


def split_points(widths):
    pts, acc = [], 0
    for w in widths[:-1]:
        acc += w
        pts.append(acc)
    return pts


def gated_deltanet(q_in, k_in, v_in, z_in, b_in, a_in, conv_w, a_log, dt_bias, norm_w):
    bsz, L, _ = q_in.shape
    qkv = jax.nn.silu(causal_conv(jnp.concatenate([q_in, k_in, v_in], -1), conv_w))
    q, k, v = jnp.split(qkv, 3, axis=-1)
    q = l2norm(q.reshape(bsz, L, DN_H, DN_DK)) * (DN_DK ** -0.5)
    k = l2norm(k.reshape(bsz, L, DN_H, DN_DK))
    v = v.reshape(bsz, L, DN_H, DN_DV)
    beta = jax.nn.sigmoid(b_in)
    g = -jnp.exp(a_log) * jax.nn.softplus(a_in + dt_bias)
    q, k, v, beta, g = (to_chunks(t) for t in (q, k, v, beta, g))
    gc = jnp.cumsum(g, axis=-1)
    causal = jnp.tril(jnp.ones((CHUNK, CHUNK), bool))
    strict = jnp.tril(jnp.ones((CHUNK, CHUNK), bool), -1)
    decay = jnp.exp(jnp.where(causal, gc[..., :, None] - gc[..., None, :], -jnp.inf))
    kb = k * beta[..., None]
    m = jnp.where(strict, jnp.einsum("bnhtd,bnhsd->bnhts", kb, k) * decay, 0.0)
    rhs = jnp.concatenate([v * beta[..., None], kb * jnp.exp(gc)[..., None]], axis=-1)
    sol = lax.linalg.triangular_solve(m, rhs, left_side=True, lower=True, unit_diagonal=True)
    u, w = sol[..., :DN_DV], sol[..., DN_DV:]
    attn = jnp.einsum("bnhtd,bnhsd->bnhts", q, k) * decay
    qg = q * jnp.exp(gc)[..., None]
    g_last = gc[..., -1]
    kdec = k * jnp.exp(g_last[..., None] - gc)[..., None]

    def step(S, xs):
        qg_n, w_n, u_n, attn_n, kdec_n, gl_n = xs
        v_new = u_n - jnp.einsum("bhtd,bhde->bhte", w_n, S)
        o_n = jnp.einsum("bhtd,bhde->bhte", qg_n, S) + jnp.einsum("bhts,bhse->bhte", attn_n, v_new)
        S = S * jnp.exp(gl_n)[..., None, None] + jnp.einsum("bhsd,bhse->bhde", kdec_n, v_new)
        return S, o_n

    xs = tuple(jnp.moveaxis(t, 1, 0) for t in (qg, w, u, attn, kdec, g_last))
    s0 = jnp.zeros((bsz, DN_H, DN_DK, DN_DV), jnp.float32)
    _, o = lax.scan(step, s0, xs)
    o = from_chunks(jnp.moveaxis(o, 0, 1))
    o = rmsnorm(o, norm_w) * jax.nn.silu(z_in.reshape(bsz, L, DN_H, DN_DV))
    return o.reshape(bsz, L, DN_H * DN_DV)


def diff_attention(q_in, k_in, v_in, lam_p, norm_w, lambda_init):
    bsz, L, _ = q_in.shape
    q = front_pad(q_in.reshape(bsz, L, DF_H, 2, DF_DQK))
    k = front_pad(k_in.reshape(bsz, L, DF_H, 2, DF_DQK))
    v = front_pad(v_in.reshape(bsz, L, DF_H, DF_DV))
    lp = q.shape[1]
    nb = lp // CHUNK
    lam = jnp.exp(jnp.sum(lam_p[0] * lam_p[1])) - jnp.exp(jnp.sum(lam_p[2] * lam_p[3])) + lambda_init
    kpos = jnp.arange(lp)
    qb = jnp.moveaxis(q.reshape(bsz, nb, CHUNK, DF_H, 2, DF_DQK), 1, 0)
    scale = DF_DQK ** -0.5

    def block(args):
        q_blk, i = args
        qpos = i * CHUNK + jnp.arange(CHUNK)
        s = jnp.einsum("bthmd,bshmd->bhmts", q_blk, k) * scale
        mask = (kpos[None, :] <= qpos[:, None]) & (kpos[None, :] >= PAD)
        p = jax.nn.softmax(jnp.where(mask, s, NEG), axis=-1)
        a = p[:, :, 0] - lam * p[:, :, 1]
        return jnp.einsum("bhts,bshe->bthe", a, v)

    o = lax.map(block, (qb, jnp.arange(nb)))
    o = jnp.moveaxis(o, 0, 1).reshape(bsz, lp, DF_H, DF_DV)[:, PAD:]
    o = rmsnorm(o, norm_w) * (1.0 - lambda_init)
    return o.reshape(bsz, L, DF_H * DF_DV)


def swa_sinks(q_in, k_in, v_in, sinks):
    bsz, L, _ = q_in.shape
    q = front_pad(q_in.reshape(bsz, L, SW_HKV, SW_REP, SW_D))
    k = front_pad(k_in.reshape(bsz, L, SW_HKV, SW_D))
    v = front_pad(v_in.reshape(bsz, L, SW_HKV, SW_D))
    lp = q.shape[1]
    nb = lp // CHUNK
    qb = q.reshape(bsz, nb, CHUNK, SW_HKV, SW_REP, SW_D)
    kb = k.reshape(bsz, nb, CHUNK, SW_HKV, SW_D)
    vb = v.reshape(bsz, nb, CHUNK, SW_HKV, SW_D)
    kk = jnp.concatenate([jnp.concatenate([jnp.zeros_like(kb[:, :1]), kb[:, :-1]], 1), kb], 2)
    vv = jnp.concatenate([jnp.concatenate([jnp.zeros_like(vb[:, :1]), vb[:, :-1]], 1), vb], 2)
    blk = jnp.arange(nb)[:, None]
    qpos = blk * CHUNK + jnp.arange(CHUNK)
    kpos = blk * CHUNK - CHUNK + jnp.arange(2 * CHUNK)
    rel = qpos[:, :, None] - kpos[:, None, :]
    mask = (rel >= 0) & (rel < WINDOW) & (kpos[:, None, :] >= PAD)
    s = jnp.einsum("bntgrd,bnsgd->bngrts", qb, kk) * (SW_D ** -0.5)
    s = jnp.where(mask[None, :, None, None], s, NEG)
    sink = jnp.broadcast_to(sinks.reshape(1, 1, SW_HKV, SW_REP, 1, 1), s.shape[:-1] + (1,))
    p = jax.nn.softmax(jnp.concatenate([s, sink], axis=-1), axis=-1)[..., :-1]
    o = jnp.einsum("bngrts,bnsgd->bntgrd", p, vv)
    return o.reshape(bsz, lp, SW_HQ * SW_D)[:, PAD:]


def mamba2_ssd(z_in, xbc_in, dt_in, conv_w, conv_b, a_log, dt_bias, d_skip, norm_w):
    bsz, L, _ = xbc_in.shape
    xbc = jax.nn.silu(causal_conv(xbc_in, conv_w) + conv_b)
    x, bm, cm = jnp.split(xbc, [D_SSM, D_SSM + SSD_G * SSD_N], axis=-1)
    x = x.reshape(bsz, L, SSD_H, SSD_P)
    rep = SSD_H // SSD_G
    bm = jnp.repeat(bm.reshape(bsz, L, SSD_G, SSD_N), rep, axis=2)
    cm = jnp.repeat(cm.reshape(bsz, L, SSD_G, SSD_N), rep, axis=2)
    dt = jax.nn.softplus(dt_in + dt_bias)
    a = -jnp.exp(a_log)
    xc, bc, cc, dtc = (to_chunks(t) for t in (x * dt[..., None], bm, cm, dt))
    acum = jnp.cumsum(dtc * a[:, None], axis=-1)
    causal = jnp.tril(jnp.ones((CHUNK, CHUNK), bool))
    lmat = jnp.exp(jnp.where(causal, acum[..., :, None] - acum[..., None, :], -jnp.inf))
    y_diag = jnp.einsum("bnhts,bnhsp->bnhtp", jnp.einsum("bnhtc,bnhsc->bnhts", cc, bc) * lmat, xc)
    states = jnp.einsum("bnhsc,bnhs,bnhsp->bnhpc", bc, jnp.exp(acum[..., -1:] - acum), xc)
    chunk_decay = jnp.exp(acum[..., -1])

    def step(hs, xs):
        st_n, dec_n = xs
        return hs * dec_n[..., None, None] + st_n, hs

    h0 = jnp.zeros((bsz, SSD_H, SSD_P, SSD_N), jnp.float32)
    _, h_prev = lax.scan(step, h0, (jnp.moveaxis(states, 1, 0), jnp.moveaxis(chunk_decay, 1, 0)))
    h_prev = jnp.moveaxis(h_prev, 0, 1)
    y_off = jnp.einsum("bnhtc,bnhpc,bnht->bnhtp", cc, h_prev, jnp.exp(acum))
    y = from_chunks(y_diag + y_off) + d_skip[:, None] * x
    y = y.reshape(bsz, L, D_SSM) * jax.nn.silu(z_in)
    y = rmsnorm(y.reshape(bsz, L, SSD_G, D_SSM // SSD_G), norm_w.reshape(SSD_G, -1))
    return y.reshape(bsz, L, D_SSM)


def swiglu(x, w_gate, w_up, w_down):
    return (jax.nn.silu(x @ w_gate) * (x @ w_up)) @ w_down


def moe_swiglu(u, router, e_gate, e_up, e_down):
    bsz, L, d = u.shape
    xt = u.reshape(-1, d)
    T = xt.shape[0]
    logits = jnp.einsum("td,de->te", xt, router).astype(jnp.float32)
    top_v, top_i = lax.top_k(logits, TOP_K)
    gates = jax.nn.softmax(top_v, axis=-1)
    n_assign = T * TOP_K
    exp_flat = top_i.reshape(-1)
    tok_flat = jnp.repeat(jnp.arange(T), TOP_K)
    gate_flat = gates.reshape(-1)
    order = jnp.argsort(exp_flat * n_assign + jnp.arange(n_assign))
    exp_s, tok_s, gate_s = exp_flat[order], tok_flat[order], gate_flat[order]
    counts = jnp.bincount(exp_flat, length=N_EXP)
    padded = ((counts + MOE_BLOCK - 1) // MOE_BLOCK) * MOE_BLOCK
    start_s = jnp.cumsum(counts) - counts
    ends_p = jnp.cumsum(padded)
    start_p = ends_p - padded
    dest = start_p[exp_s] + (jnp.arange(n_assign) - start_s[exp_s])
    cap = (-(-n_assign // MOE_BLOCK) + N_EXP) * MOE_BLOCK
    buf_tok = jnp.zeros((cap,), jnp.int32).at[dest].set(tok_s)
    buf_gate = jnp.zeros((cap,), jnp.float32).at[dest].set(gate_s)
    nblk = cap // MOE_BLOCK
    blk_start = jnp.arange(nblk) * MOE_BLOCK
    blk_exp = jnp.minimum(jnp.sum(ends_p[None, :] <= blk_start[:, None], axis=-1), N_EXP - 1)
    xb = xt[buf_tok].reshape(nblk, MOE_BLOCK, d)

    def expert_block(args):
        x_blk, e = args
        return swiglu(x_blk, e_gate[e], e_up[e], e_down[e])

    yb = lax.map(expert_block, (xb, blk_exp)).reshape(cap, d)
    y = jnp.zeros((T, d), u.dtype).at[buf_tok].add((yb * buf_gate[:, None]).astype(u.dtype))
    return y.reshape(bsz, L, d)


def setup_inputs(seed: int = 0) -> dict:
    key = jax.random.key(seed)
    ks = iter(jax.random.split(key, 40))

    def nrm(shape, scale=1.0):
        return jax.random.normal(next(ks), shape, jnp.float32) * scale

    def gain(shape):
        return 1.0 + 0.01 * jax.random.normal(next(ks), shape, jnp.float32)

    def a_log(shape):
        return jnp.log(jax.random.uniform(next(ks), shape, jnp.float32, 1.0, 16.0))

    def dt_bias(shape):
        lo, hi = math.log(1e-3), math.log(1e-1)
        dt = jnp.exp(jax.random.uniform(next(ks), shape, jnp.float32) * (hi - lo) + lo)
        return dt + jnp.log(-jnp.expm1(-dt))

    conv_w_dn = 3 * DN_H * DN_DK
    conv_w_ssd = D_SSM + 2 * SSD_G * SSD_N
    return {
        "x": nrm((BATCH, SEQ, D_MODEL)),
        "meta_tokens": nrm((N_META, D_MODEL)),
        "norm_mix": gain((DEPTH, D_MODEL)),
        "w_in": nrm((DEPTH, D_MODEL, N_COLS), D_MODEL ** -0.5),
        "dn_conv_w": nrm((DEPTH, CONV_K, conv_w_dn), CONV_K ** -0.5),
        "dn_a_log": a_log((DEPTH, DN_H)),
        "dn_dt_bias": dt_bias((DEPTH, DN_H)),
        "dn_norm_w": gain((DEPTH, DN_DV)),
        "df_lambda": nrm((DEPTH, 4, DF_DQK), 0.1),
        "df_norm_w": gain((DEPTH, DF_DV)),
        "sw_sinks": nrm((DEPTH, SW_HQ), 0.5),
        "ssd_conv_w": nrm((DEPTH, CONV_K, conv_w_ssd), CONV_K ** -0.5),
        "ssd_conv_b": nrm((DEPTH, conv_w_ssd), 0.01),
        "ssd_a_log": a_log((DEPTH, SSD_H)),
        "ssd_dt_bias": dt_bias((DEPTH, SSD_H)),
        "ssd_d": gain((DEPTH, SSD_H)),
        "ssd_norm_w": gain((DEPTH, D_SSM)),
        "w_out": nrm((DEPTH, D_MIX, D_MODEL), D_MIX ** -0.5),
        "norm_ffn": gain((DEPTH, D_MODEL)),
        "ffn_w_gate": nrm((N_DENSE, D_MODEL, D_FF), D_MODEL ** -0.5),
        "ffn_w_up": nrm((N_DENSE, D_MODEL, D_FF), D_MODEL ** -0.5),
        "ffn_w_down": nrm((N_DENSE, D_FF, D_MODEL), D_FF ** -0.5),
        "moe_router": nrm((N_MOE, D_MODEL, N_EXP), D_MODEL ** -0.5),
        "moe_w_gate": nrm((N_MOE, N_EXP, D_MODEL, D_FF_E), D_MODEL ** -0.5),
        "moe_w_up": nrm((N_MOE, N_EXP, D_MODEL, D_FF_E), D_MODEL ** -0.5),
        "moe_w_down": nrm((N_MOE, N_EXP, D_FF_E, D_MODEL), D_FF_E ** -0.5),
        "norm_final": gain((D_MODEL,)),
    }


def reference(x, meta_tokens, norm_mix, w_in, dn_conv_w, dn_a_log, dn_dt_bias, dn_norm_w,
              df_lambda, df_norm_w, sw_sinks, ssd_conv_w, ssd_conv_b, ssd_a_log, ssd_dt_bias,
              ssd_d, ssd_norm_w, w_out, norm_ffn, ffn_w_gate, ffn_w_up, ffn_w_down,
              moe_router, moe_w_gate, moe_w_up, moe_w_down, norm_final):
    bsz = x.shape[0]
    f32 = jnp.float32
    meta = jnp.broadcast_to(meta_tokens[None].astype(x.dtype), (bsz, N_META, D_MODEL))
    h = jnp.concatenate([meta, x], axis=1)
    pts = split_points(COL_WIDTHS)
    for l in range(DEPTH):
        u = rmsnorm(h, norm_mix[l])
        proj = jnp.einsum("bld,dc->blc", u, w_in[l]).astype(f32)
        (dn_q, dn_k, dn_v, dn_z, dn_b, dn_a, df_q, df_k, df_v,
         sw_q, sw_k, sw_v, ssd_z, ssd_xbc, ssd_dt) = jnp.split(proj, pts, axis=-1)
        y_dn = gated_deltanet(dn_q, dn_k, dn_v, dn_z, dn_b, dn_a, dn_conv_w[l].astype(f32),
                              dn_a_log[l].astype(f32), dn_dt_bias[l].astype(f32), dn_norm_w[l])
        lambda_init = 0.8 - 0.6 * math.exp(-0.3 * l)
        y_df = diff_attention(df_q, df_k, df_v, df_lambda[l].astype(f32), df_norm_w[l], lambda_init)
        y_sw = swa_sinks(sw_q, sw_k, sw_v, sw_sinks[l].astype(f32))
        y_ssd = mamba2_ssd(ssd_z, ssd_xbc, ssd_dt, ssd_conv_w[l].astype(f32), ssd_conv_b[l].astype(f32),
                           ssd_a_log[l].astype(f32), ssd_dt_bias[l].astype(f32),
                           ssd_d[l].astype(f32), ssd_norm_w[l])
        mix = jnp.concatenate([y_dn, y_df, y_sw, y_ssd], axis=-1).astype(h.dtype)
        h = h + jnp.einsum("blc,cd->bld", mix, w_out[l])
        u = rmsnorm(h, norm_ffn[l])
        if l % 2 == 0:
            i = l // 2
            h = h + swiglu(u, ffn_w_gate[i], ffn_w_up[i], ffn_w_down[i])
        else:
            i = l // 2
            h = h + moe_swiglu(u, moe_router[i], moe_w_gate[i], moe_w_up[i], moe_w_down[i])
    y = rmsnorm(h, norm_final)
    return y[:, N_META:]
```

```python
import functools
import math

import jax
import jax.numpy as jnp
from jax import lax
from jax.experimental import pallas as pl
from jax.experimental.pallas import tpu as pltpu

F32 = jnp.float32
BF16 = jnp.bfloat16

N_META = 16
CHUNK = 128
PAD = CHUNK - N_META
EPS = 1e-6
NEG = -1e30
CONV_K = 4

DN_H, DN_D = 4, 64
DF_H, DF_DQK, DF_DV = 4, 32, 64
SW_HQ, SW_HKV, SW_D = 4, 2, 64
SSD_H, SSD_P, SSD_N, SSD_G = 4, 64, 128, 2
N_EXP = 8

SM_DN_B, SM_DN_A, SM_SSD_DT = 0, 4, 8

VMEM_LIMIT = 56 * 1024 * 1024

TM_PROJ = 256
TM_OUT = 512
TM_FFN = 512
TF_DENSE = 1408
TM_MOE = 512
TF_MOE = 896
TG_GATHER = 512

HI = lax.Precision.HIGHEST


def _dot(a, b):
    return jnp.dot(a.astype(BF16), b.astype(BF16), preferred_element_type=F32)


def _dot_nt(a, b):
    return lax.dot_general(a.astype(BF16), b.astype(BF16), (((1,), (1,)), ((), ())),
                           preferred_element_type=F32)


def _dot_tn(a, b):
    return lax.dot_general(a.astype(BF16), b.astype(BF16), (((0,), (0,)), ((), ())),
                           preferred_element_type=F32)


def _dot_hi(a, b):
    return jnp.dot(a, b, precision=HI, preferred_element_type=F32)


def _sigmoid(x):
    return 1.0 / (1.0 + jnp.exp(-x))


def _silu(x):
    return x * _sigmoid(x)


def _softplus(x):
    return jnp.maximum(x, 0.0) + jnp.log1p(jnp.exp(-jnp.abs(x)))


def _rms(x, w):
    return x * lax.rsqrt(jnp.mean(x * x, axis=-1, keepdims=True) + EPS) * w


def _iota(shape, dim):
    return lax.broadcasted_iota(jnp.int32, shape, dim)


def _causal_conv(buf, x, cw, first):
    width = x.shape[1]

    @pl.when(first)
    def _():
        buf[0:8, :] = jnp.zeros((8, width), F32)

    buf[8:8 + CHUNK, :] = x
    conv = cw[CONV_K - 1:CONV_K, :] * x
    for j in range(CONV_K - 1):
        lo = 8 - (CONV_K - 1) + j
        conv = conv + cw[j:j + 1, :] * buf[lo:lo + CHUNK, :]
    buf[0:8, :] = buf[CHUNK:CHUNK + 8, :]
    return conv


def _inproj_kernel(h_ref, nw_ref, w_ref, *out_refs, widths):
    u = _rms(h_ref[...], nw_ref[...])
    p = jnp.dot(u.astype(BF16), w_ref[...], preferred_element_type=F32)
    off = 0
    for o_ref, w in zip(out_refs, widths):
        o_ref[...] = p[:, off:off + w]
        off += w


PROJ_WIDTHS = (768, 256, 128, 256, 256, 256, 256, 256, 256, 768)


def _inproj(h, nw, w):
    rows, d = h.shape
    ncol = w.shape[1]
    tm = TM_PROJ
    return pl.pallas_call(
        functools.partial(_inproj_kernel, widths=PROJ_WIDTHS),
        grid=(rows // tm,),
        in_specs=[pl.BlockSpec((tm, d), lambda i: (i, 0)),
                  pl.BlockSpec((1, d), lambda i: (0, 0)),
                  pl.BlockSpec((d, ncol), lambda i: (0, 0))],
        out_specs=[pl.BlockSpec((tm, wd), lambda i: (i, 0)) for wd in PROJ_WIDTHS],
        out_shape=[jax.ShapeDtypeStruct((rows, wd), F32) for wd in PROJ_WIDTHS],
        compiler_params=pltpu.CompilerParams(dimension_semantics=("arbitrary",),
                                             vmem_limit_bytes=VMEM_LIMIT),
        name="inproj",
    )(h, nw, w)


def _dn_kernel(qkv_ref, z_ref, sm_ref, cw_ref, alog_ref, dtb_ref, nw_ref, o_ref, buf, s_ref):
    c = pl.program_id(1)
    first = c == 0

    @pl.when(first)
    def _():
        s_ref[...] = jnp.zeros_like(s_ref)

    qkv = _silu(_causal_conv(buf, qkv_ref[...], cw_ref[...], first))
    hd = DN_H * DN_D

    row1 = _iota((CHUNK, 1), 0)
    valid = jnp.logical_or(row1 >= PAD, c > 0)
    row = _iota((CHUNK, CHUNK), 0)
    col = _iota((CHUNK, CHUNK), 1)
    causal = col <= row
    strict = col < row
    ltri = causal.astype(F32)

    sm = sm_ref[...]
    lane = _iota((1, CHUNK), 1)
    glane = jnp.logical_and(lane >= SM_DN_A, lane < SM_DN_A + DN_H)
    g_tile = -jnp.exp(alog_ref[...]) * _softplus(sm + dtb_ref[...])
    g_tile = jnp.where(jnp.logical_and(glane, valid), g_tile, 0.0)
    gc_all = _dot_hi(ltri, g_tile)
    gc_all_t = gc_all.T

    z = z_ref[...]
    nw = nw_ref[...]
    outs = []
    for h in range(DN_H):
        lo = h * DN_D
        q = qkv[:, lo:lo + DN_D]
        k = qkv[:, hd + lo:hd + lo + DN_D]
        v = qkv[:, 2 * hd + lo:2 * hd + lo + DN_D]
        q = q * lax.rsqrt(jnp.sum(q * q, axis=-1, keepdims=True) + EPS) * (DN_D ** -0.5)
        k = k * lax.rsqrt(jnp.sum(k * k, axis=-1, keepdims=True) + EPS)
        beta = _sigmoid(sm[:, SM_DN_B + h:SM_DN_B + h + 1])
        gl = SM_DN_A + h
        gc = gc_all[:, gl:gl + 1]
        gc_row = gc_all_t[gl:gl + 1, :]
        g_last = gc_all[CHUNK - 1:CHUNK, gl:gl + 1]
        decay = jnp.exp(jnp.where(causal, gc - gc_row, NEG))
        kb = k * beta
        a = jnp.where(strict, _dot_nt(kb, k) * decay, 0.0)
        egc = jnp.exp(gc)
        sol = jnp.concatenate([v * beta, kb * egc], axis=-1)
        pw = -a
        for j in range(7):
            sol = sol + _dot_hi(pw, sol)
            if j < 6:
                pw = _dot_hi(pw, pw)
        u = sol[:, :DN_D]
        w = sol[:, DN_D:]
        attn = _dot_nt(q, k) * decay
        s_old = s_ref[h]
        v_new = u - _dot(w, s_old)
        o = _dot(q * egc, s_old) + _dot(attn, v_new)
        kdec = k * jnp.exp(g_last - gc)
        s_ref[h] = s_old * jnp.exp(g_last) + _dot_tn(kdec, v_new)
        o = _rms(o, nw) * _silu(z[:, lo:lo + DN_D])
        outs.append(jnp.where(valid, o, 0.0))
    o_ref[...] = jnp.concatenate(outs, axis=-1)


def _deltanet(qkv, z, sm, cw, alog_row, dtb_row, nw, bsz, nc):
    rows = qkv.shape[0]
    rb = lambda b, c: (b * nc + c, 0)
    fixed = lambda b, c: (0, 0)
    return pl.pallas_call(
        _dn_kernel,
        grid=(bsz, nc),
        in_specs=[pl.BlockSpec((CHUNK, 3 * DN_H * DN_D), rb),
                  pl.BlockSpec((CHUNK, DN_H * DN_D), rb),
                  pl.BlockSpec((CHUNK, CHUNK), rb),
                  pl.BlockSpec((CONV_K, 3 * DN_H * DN_D), fixed),
                  pl.BlockSpec((1, CHUNK), fixed),
                  pl.BlockSpec((1, CHUNK), fixed),
                  pl.BlockSpec((1, DN_D), fixed)],
        out_specs=pl.BlockSpec((CHUNK, DN_H * DN_D), rb),
        out_shape=jax.ShapeDtypeStruct((rows, DN_H * DN_D), F32),
        scratch_shapes=[pltpu.VMEM((CHUNK + 8, 3 * DN_H * DN_D), F32),
                        pltpu.VMEM((DN_H, DN_D, DN_D), F32)],
        compiler_params=pltpu.CompilerParams(dimension_semantics=("arbitrary", "arbitrary")),
        name="deltanet",
    )(qkv, z, sm, cw, alog_row, dtb_row, nw)


def _df_kernel(q_ref, k_ref, v_ref, lam_ref, nw_ref, o_ref, m_scr, l_scr, acc_scr, *, lambda_init):
    qi = pl.program_id(1)
    m_scr[...] = jnp.full(m_scr.shape, NEG, F32)
    l_scr[...] = jnp.zeros(l_scr.shape, F32)
    acc_scr[...] = jnp.zeros(acc_scr.shape, F32)
    q = q_ref[...] * (DF_DQK ** -0.5)
    row = _iota((CHUNK, CHUNK), 0)
    col = _iota((CHUNK, CHUNK), 1)

    def tile(j, masked):
        off = pl.multiple_of(j * CHUNK, CHUNK)
        kt = k_ref[pl.ds(off, CHUNK), :]
        vt = v_ref[pl.ds(off, CHUNK), :]
        if masked:
            kpos = j * CHUNK + col
            msk = jnp.logical_and(kpos <= qi * CHUNK + row, kpos >= PAD)
        for h in range(DF_H):
            vh = vt[:, h * DF_DV:(h + 1) * DF_DV]
            for m in range(2):
                idx = 2 * h + m
                lo = h * 2 * DF_DQK + m * DF_DQK
                s = _dot_nt(q[:, lo:lo + DF_DQK], kt[:, lo:lo + DF_DQK])
                if masked:
                    s = jnp.where(msk, s, NEG)
                m_old = m_scr[idx]
                m_new = jnp.maximum(m_old, jnp.max(s, axis=-1, keepdims=True))
                alpha = jnp.exp(m_old - m_new)
                p = jnp.exp(s - m_new)
                l_scr[idx] = alpha * l_scr[idx] + jnp.sum(p, axis=-1, keepdims=True)
                acc_scr[idx] = alpha * acc_scr[idx] + _dot(p, vh)
                m_scr[idx] = m_new

    tile(0, True)

    def body(j, carry):
        tile(j, False)
        return carry

    lax.fori_loop(1, qi, body, 0)

    @pl.when(qi > 0)
    def _():
        tile(qi, True)

    lp = lam_ref[...]
    lam = (jnp.exp(jnp.sum(lp[0:1] * lp[1:2], axis=-1, keepdims=True))
           - jnp.exp(jnp.sum(lp[2:3] * lp[3:4], axis=-1, keepdims=True)) + lambda_init)
    valid = jnp.logical_or(_iota((CHUNK, 1), 0) >= PAD, qi > 0)
    nw = nw_ref[...]
    outs = []
    for h in range(DF_H):
        o = acc_scr[2 * h] / l_scr[2 * h] - lam * (acc_scr[2 * h + 1] / l_scr[2 * h + 1])
        o = _rms(o, nw) * (1.0 - lambda_init)
        outs.append(jnp.where(valid, o, 0.0))
    o_ref[...] = jnp.concatenate(outs, axis=-1)


def _diff_attention(q, k, v, lam_p, nw, lambda_init, bsz, nc):
    rows, width = q.shape
    lp = nc * CHUNK
    return pl.pallas_call(
        functools.partial(_df_kernel, lambda_init=lambda_init),
        grid=(bsz, nc),
        in_specs=[pl.BlockSpec((CHUNK, width), lambda b, c: (b * nc + c, 0)),
                  pl.BlockSpec((lp, width), lambda b, c: (b, 0)),
                  pl.BlockSpec((lp, width), lambda b, c: (b, 0)),
                  pl.BlockSpec((4, DF_DQK), lambda b, c: (0, 0)),
                  pl.BlockSpec((1, DF_DV), lambda b, c: (0, 0))],
        out_specs=pl.BlockSpec((CHUNK, width), lambda b, c: (b * nc + c, 0)),
        out_shape=jax.ShapeDtypeStruct((rows, width), F32),
        scratch_shapes=[pltpu.VMEM((2 * DF_H, CHUNK, 1), F32),
                        pltpu.VMEM((2 * DF_H, CHUNK, 1), F32),
                        pltpu.VMEM((2 * DF_H, CHUNK, DF_DV), F32)],
        compiler_params=pltpu.CompilerParams(dimension_semantics=("arbitrary", "arbitrary"),
                                             vmem_limit_bytes=VMEM_LIMIT),
        name="diff_attention",
    )(q, k, v, lam_p, nw)


def _sw_kernel(sink_ref, q_ref, kv_ref, kvp_ref, o_ref):
    c = pl.program_id(1)
    q = q_ref[...] * (SW_D ** -0.5)
    kv = kv_ref[...]
    kvp = kvp_ref[...]
    row = _iota((CHUNK, 2 * CHUNK), 0)
    col = _iota((CHUNK, 2 * CHUNK), 1)
    rel = row - col + CHUNK
    kpos = c * CHUNK - CHUNK + col
    msk = jnp.logical_and(jnp.logical_and(rel >= 0, rel < CHUNK), kpos >= PAD)
    valid = jnp.logical_or(_iota((CHUNK, 1), 0) >= PAD, c > 0)
    kvw = SW_HKV * SW_D
    outs = []
    for g in range(SW_HKV):
        kk = jnp.concatenate([kvp[:, g * SW_D:(g + 1) * SW_D], kv[:, g * SW_D:(g + 1) * SW_D]], axis=0)
        vv = jnp.concatenate([kvp[:, kvw + g * SW_D:kvw + (g + 1) * SW_D],
                              kv[:, kvw + g * SW_D:kvw + (g + 1) * SW_D]], axis=0)
        for r in range(SW_HQ // SW_HKV):
            hq = g * (SW_HQ // SW_HKV) + r
            sink = sink_ref[hq]
            s = jnp.where(msk, _dot_nt(q[:, hq * SW_D:(hq + 1) * SW_D], kk), NEG)
            m = jnp.maximum(jnp.max(s, axis=-1, keepdims=True), sink)
            p = jnp.exp(s - m)
            denom = jnp.sum(p, axis=-1, keepdims=True) + jnp.exp(sink - m)
            o = _dot(p, vv) / denom
            outs.append(jnp.where(valid, o, 0.0))
    o_ref[...] = jnp.concatenate(outs, axis=-1)


def _swa(q, kv, sinks, bsz, nc):
    rows, width = q.shape
    return pl.pallas_call(
        _sw_kernel,
        grid=(bsz, nc),
        in_specs=[pl.BlockSpec(memory_space=pltpu.SMEM),
                  pl.BlockSpec((CHUNK, width), lambda b, c: (b * nc + c, 0)),
                  pl.BlockSpec((CHUNK, width), lambda b, c: (b * nc + c, 0)),
                  pl.BlockSpec((CHUNK, width), lambda b, c: (b * nc + jnp.maximum(c - 1, 0), 0))],
        out_specs=pl.BlockSpec((CHUNK, width), lambda b, c: (b * nc + c, 0)),
        out_shape=jax.ShapeDtypeStruct((rows, width), F32),
        compiler_params=pltpu.CompilerParams(dimension_semantics=("arbitrary", "arbitrary")),
        name="swa",
    )(sinks, q, kv, kv)


def _ssd_kernel(d_ref, xbc_ref, z_ref, sm_ref, cw_ref, cb_ref, alog_ref, dtb_ref, nw_ref, o_ref,
                buf, h_ref):
    c = pl.program_id(1)
    first = c == 0

    @pl.when(first)
    def _():
        h_ref[...] = jnp.zeros_like(h_ref)

    valid = jnp.logical_or(_iota((CHUNK, 1), 0) >= PAD, c > 0)
    conv = _causal_conv(buf, xbc_ref[...], cw_ref[...], first) + cb_ref[...]
    xbc = jnp.where(valid, _silu(conv), 0.0)
    d_ssm = SSD_H * SSD_P
    gw = SSD_G * SSD_N
    x = xbc[:, :d_ssm]
    bm = xbc[:, d_ssm:d_ssm + gw]
    cm = xbc[:, d_ssm + gw:]

    row = _iota((CHUNK, CHUNK), 0)
    col = _iota((CHUNK, CHUNK), 1)
    causal = col <= row
    ltri = causal.astype(F32)
    lane = _iota((1, CHUNK), 1)
    dlane = jnp.logical_and(lane >= SM_SSD_DT, lane < SM_SSD_DT + SSD_H)
    dt_tile = jnp.where(jnp.logical_and(dlane, valid), _softplus(sm_ref[...] + dtb_ref[...]), 0.0)
    da_tile = dt_tile * (-jnp.exp(alog_ref[...]))
    ac_all = _dot_hi(ltri, da_tile)
    ac_all_t = ac_all.T

    hp = SSD_H // SSD_G
    h_old = h_ref[...]
    ys, hs = [], []
    for g in range(SSD_G):
        bg = bm[:, g * SSD_N:(g + 1) * SSD_N]
        cg = cm[:, g * SSD_N:(g + 1) * SSD_N]
        cb = _dot_nt(cg, bg)
        for r in range(hp):
            h = g * hp + r
            al = SM_SSD_DT + h
            dt = dt_tile[:, al:al + 1]
            ac = ac_all[:, al:al + 1]
            ac_row = ac_all_t[al:al + 1, :]
            a_last = ac_all[CHUNK - 1:CHUNK, al:al + 1]
            xh = x[:, h * SSD_P:(h + 1) * SSD_P]
            xdt = xh * dt
            lmat = jnp.exp(jnp.where(causal, ac - ac_row, NEG))
            hprev = h_old[:, h * SSD_P:(h + 1) * SSD_P]
            y = _dot(cb * lmat, xdt) + _dot(cg, hprev) * jnp.exp(ac) + d_ref[h] * xh
            ys.append(y)
            xdec = xdt * jnp.exp(a_last - ac)
            hs.append(hprev * jnp.exp(a_last) + _dot_tn(bg, xdec))
    h_ref[...] = jnp.concatenate(hs, axis=-1)
    y = jnp.concatenate(ys, axis=-1) * _silu(z_ref[...])
    nw = nw_ref[...]
    gwid = d_ssm // SSD_G
    outs = [_rms(y[:, g * gwid:(g + 1) * gwid], nw[:, g * gwid:(g + 1) * gwid]) for g in range(SSD_G)]
    o_ref[...] = jnp.where(valid, jnp.concatenate(outs, axis=-1), 0.0)


def _ssd(xbc, z, sm, cw, cb, alog_row, dtb_row, d_skip, nw, bsz, nc):
    rows, cwid = xbc.shape
    d_ssm = SSD_H * SSD_P
    rb = lambda b, c: (b * nc + c, 0)
    fixed = lambda b, c: (0, 0)
    return pl.pallas_call(
        _ssd_kernel,
        grid=(bsz, nc),
        in_specs=[pl.BlockSpec(memory_space=pltpu.SMEM),
                  pl.BlockSpec((CHUNK, cwid), rb),
                  pl.BlockSpec((CHUNK, d_ssm), rb),
                  pl.BlockSpec((CHUNK, CHUNK), rb),
                  pl.BlockSpec((CONV_K, cwid), fixed),
                  pl.BlockSpec((1, cwid), fixed),
                  pl.BlockSpec((1, CHUNK), fixed),
                  pl.BlockSpec((1, CHUNK), fixed),
                  pl.BlockSpec((1, d_ssm), fixed)],
        out_specs=pl.BlockSpec((CHUNK, d_ssm), rb),
        out_shape=jax.ShapeDtypeStruct((rows, d_ssm), F32),
        scratch_shapes=[pltpu.VMEM((CHUNK + 8, cwid), F32),
                        pltpu.VMEM((SSD_N, d_ssm), F32)],
        compiler_params=pltpu.CompilerParams(dimension_semantics=("arbitrary", "arbitrary")),
        name="ssd",
    )(d_skip, xbc, z, sm, cw, cb, alog_row, dtb_row, nw)


def _outproj_kernel(h_ref, y0_ref, y1_ref, y2_ref, y3_ref, w_ref, o_ref):
    acc = h_ref[...]
    for i, y_ref in enumerate((y0_ref, y1_ref, y2_ref, y3_ref)):
        wd = y_ref.shape[1]
        acc = acc + jnp.dot(y_ref[...].astype(BF16), w_ref[i * wd:(i + 1) * wd, :],
                            preferred_element_type=F32)
    o_ref[...] = acc


def _outproj(h, ys, w):
    rows, d = h.shape
    tm = TM_OUT
    wd = ys[0].shape[1]
    return pl.pallas_call(
        _outproj_kernel,
        grid=(rows // tm,),
        in_specs=[pl.BlockSpec((tm, d), lambda i: (i, 0))]
                 + [pl.BlockSpec((tm, wd), lambda i: (i, 0)) for _ in ys]
                 + [pl.BlockSpec(w.shape, lambda i: (0, 0))],
        out_specs=pl.BlockSpec((tm, d), lambda i: (i, 0)),
        out_shape=jax.ShapeDtypeStruct((rows, d), F32),
        compiler_params=pltpu.CompilerParams(dimension_semantics=("arbitrary",),
                                             vmem_limit_bytes=VMEM_LIMIT),
        name="outproj",
    )(h, *ys, w)


def _ffn_kernel(h_ref, nw_ref, wg_ref, wu_ref, wd_ref, o_ref, u_scr, acc_scr):
    f = pl.program_id(1)

    @pl.when(f == 0)
    def _():
        u_scr[...] = _rms(h_ref[...], nw_ref[...]).astype(BF16)
        acc_scr[...] = jnp.zeros_like(acc_scr)

    u = u_scr[...]
    g = jnp.dot(u, wg_ref[...], preferred_element_type=F32)
    up = jnp.dot(u, wu_ref[...], preferred_element_type=F32)
    a = (_silu(g) * up).astype(BF16)
    acc_scr[...] += jnp.dot(a, wd_ref[...], preferred_element_type=F32)

    @pl.when(f == pl.num_programs(1) - 1)
    def _():
        o_ref[...] = h_ref[...] + acc_scr[...]


def _ffn(h, nw, wg, wu, wd):
    rows, d = h.shape
    dff = wg.shape[1]
    tm, tf = TM_FFN, TF_DENSE
    return pl.pallas_call(
        _ffn_kernel,
        grid=(rows // tm, dff // tf),
        in_specs=[pl.BlockSpec((tm, d), lambda i, f: (i, 0)),
                  pl.BlockSpec((1, d), lambda i, f: (0, 0)),
                  pl.BlockSpec((d, tf), lambda i, f: (0, f)),
                  pl.BlockSpec((d, tf), lambda i, f: (0, f)),
                  pl.BlockSpec((tf, d), lambda i, f: (f, 0))],
        out_specs=pl.BlockSpec((tm, d), lambda i, f: (i, 0)),
        out_shape=jax.ShapeDtypeStruct((rows, d), F32),
        scratch_shapes=[pltpu.VMEM((tm, d), BF16), pltpu.VMEM((tm, d), F32)],
        compiler_params=pltpu.CompilerParams(dimension_semantics=("arbitrary", "arbitrary"),
                                             vmem_limit_bytes=VMEM_LIMIT),
        name="ffn_dense",
    )(h, nw, wg, wu, wd)


def _router_kernel(h_ref, nw_ref, wr_ref, u_ref, r_ref):
    u = _rms(h_ref[...], nw_ref[...])
    u_ref[...] = u
    lane = _iota((1, CHUNK), 1)
    logits = jnp.where(lane < N_EXP, _dot_hi(u, wr_ref[...]), -jnp.inf)
    m1 = jnp.max(logits, axis=-1, keepdims=True)
    i1 = jnp.min(jnp.where(logits == m1, lane, CHUNK), axis=-1, keepdims=True)
    rest = jnp.where(lane == i1, -jnp.inf, logits)
    m2 = jnp.max(rest, axis=-1, keepdims=True)
    i2 = jnp.min(jnp.where(rest == m2, lane, CHUNK), axis=-1, keepdims=True)
    e = jnp.exp(m2 - m1)
    g1 = 1.0 / (1.0 + e)
    g2 = e / (1.0 + e)
    out = jnp.where(lane == 0, i1.astype(F32), 0.0)
    out = jnp.where(lane == 1, i2.astype(F32), out)
    out = jnp.where(lane == 2, g1, out)
    out = jnp.where(lane == 3, g2, out)
    r_ref[...] = out


def _router(h, nw, wr):
    rows, d = h.shape
    tm = TM_OUT
    return pl.pallas_call(
        _router_kernel,
        grid=(rows // tm,),
        in_specs=[pl.BlockSpec((tm, d), lambda i: (i, 0)),
                  pl.BlockSpec((1, d), lambda i: (0, 0)),
                  pl.BlockSpec((d, CHUNK), lambda i: (0, 0))],
        out_specs=[pl.BlockSpec((tm, d), lambda i: (i, 0)),
                   pl.BlockSpec((tm, CHUNK), lambda i: (i, 0))],
        out_shape=[jax.ShapeDtypeStruct((rows, d), F32),
                   jax.ShapeDtypeStruct((rows, CHUNK), F32)],
        compiler_params=pltpu.CompilerParams(dimension_semantics=("arbitrary",)),
        name="moe_router",
    )(h, nw, wr)


def _gather_kernel(tok_ref, u_hbm, o_ref, buf, sem):
    tg = buf.shape[0]
    base = pl.program_id(0) * tg

    def issue(r, carry):
        t = tok_ref[base + r]
        pltpu.make_async_copy(u_hbm.at[pl.ds(t, 1), :], buf.at[pl.ds(r, 1), :], sem).start()
        return carry

    lax.fori_loop(0, tg, issue, 0)
    pltpu.make_async_copy(u_hbm.at[pl.ds(0, tg), :], buf, sem).wait()
    o_ref[...] = buf[...].astype(BF16)


def _gather_rows(tok, u, cap):
    d = u.shape[1]
    tg = TG_GATHER
    return pl.pallas_call(
        _gather_kernel,
        grid_spec=pltpu.PrefetchScalarGridSpec(
            num_scalar_prefetch=1,
            grid=(cap // tg,),
            in_specs=[pl.BlockSpec(memory_space=pl.ANY)],
            out_specs=pl.BlockSpec((tg, d), lambda i, tok: (i, 0)),
            scratch_shapes=[pltpu.VMEM((tg, d), F32), pltpu.SemaphoreType.DMA(())]),
        out_shape=jax.ShapeDtypeStruct((cap, d), BF16),
        compiler_params=pltpu.CompilerParams(dimension_semantics=("arbitrary",)),
        name="moe_gather",
    )(tok, u)


def _expert_kernel(be_ref, nu_ref, x_ref, wg_ref, wu_ref, wd_ref, o_ref, acc_scr):
    i = pl.program_id(0)
    f = pl.program_id(1)
    last = f == pl.num_programs(1) - 1
    used = i < nu_ref[0]

    @pl.when(jnp.logical_and(used, f == 0))
    def _():
        acc_scr[...] = jnp.zeros_like(acc_scr)

    @pl.when(used)
    def _():
        x = x_ref[...]
        g = jnp.dot(x, wg_ref[0], preferred_element_type=F32)
        up = jnp.dot(x, wu_ref[0], preferred_element_type=F32)
        a = (_silu(g) * up).astype(BF16)
        acc_scr[...] += jnp.dot(a, wd_ref[0], preferred_element_type=F32)

    @pl.when(jnp.logical_and(used, last))
    def _():
        o_ref[...] = acc_scr[...]

    @pl.when(jnp.logical_and(jnp.logical_not(used), last))
    def _():
        o_ref[...] = jnp.zeros_like(o_ref)


def _experts(blk_exp, n_used, xb, wg, wu, wd):
    cap, d = xb.shape
    dff = wg.shape[2]
    tm, tf = TM_MOE, TF_MOE
    return pl.pallas_call(
        _expert_kernel,
        grid_spec=pltpu.PrefetchScalarGridSpec(
            num_scalar_prefetch=2,
            grid=(cap // tm, dff // tf),
            in_specs=[pl.BlockSpec((tm, d), lambda i, f, be, nu: (i, 0)),
                      pl.BlockSpec((1, d, tf), lambda i, f, be, nu: (be[i], 0, f)),
                      pl.BlockSpec((1, d, tf), lambda i, f, be, nu: (be[i], 0, f)),
                      pl.BlockSpec((1, tf, d), lambda i, f, be, nu: (be[i], f, 0))],
            out_specs=pl.BlockSpec((tm, d), lambda i, f, be, nu: (i, 0)),
            scratch_shapes=[pltpu.VMEM((tm, d), F32)]),
        out_shape=jax.ShapeDtypeStruct((cap, d), F32),
        compiler_params=pltpu.CompilerParams(dimension_semantics=("arbitrary", "arbitrary"),
                                             vmem_limit_bytes=VMEM_LIMIT),
        name="moe_experts",
    )(blk_exp, n_used, xb, wg, wu, wd)


def _combine_kernel(d1_ref, d2_ref, h_ref, r_ref, nw_ref, yb_hbm, o_ref, buf1, buf2, sem, *, nc):
    b = pl.program_id(0)
    c = pl.program_id(1)
    base = (b * nc + c + 1) * CHUNK

    def issue(r, carry):
        pltpu.make_async_copy(yb_hbm.at[pl.ds(d1_ref[base + r], 1), :], buf1.at[pl.ds(r, 1), :], sem).start()
        pltpu.make_async_copy(yb_hbm.at[pl.ds(d2_ref[base + r], 1), :], buf2.at[pl.ds(r, 1), :], sem).start()
        return carry

    lax.fori_loop(0, CHUNK, issue, 0)
    pltpu.make_async_copy(yb_hbm.at[pl.ds(0, CHUNK), :], buf1, sem).wait()
    pltpu.make_async_copy(yb_hbm.at[pl.ds(0, CHUNK), :], buf2, sem).wait()
    r = r_ref[...]
    y = h_ref[...] + r[:, 2:3] * buf1[...] + r[:, 3:4] * buf2[...]
    o_ref[0] = _rms(y, nw_ref[...])


def _combine_final(d1, d2, h, route, nw, yb, bsz, nc):
    d = h.shape[1]
    seq = (nc - 1) * CHUNK
    rb = lambda b, c, d1, d2: (b * nc + c + 1, 0)
    return pl.pallas_call(
        functools.partial(_combine_kernel, nc=nc),
        grid_spec=pltpu.PrefetchScalarGridSpec(
            num_scalar_prefetch=2,
            grid=(bsz, nc - 1),
            in_specs=[pl.BlockSpec((CHUNK, d), rb),
                      pl.BlockSpec((CHUNK, CHUNK), rb),
                      pl.BlockSpec((1, d), lambda b, c, d1, d2: (0, 0)),
                      pl.BlockSpec(memory_space=pl.ANY)],
            out_specs=pl.BlockSpec((1, CHUNK, d), lambda b, c, d1, d2: (b, c, 0)),
            scratch_shapes=[pltpu.VMEM((CHUNK, d), F32), pltpu.VMEM((CHUNK, d), F32),
                            pltpu.SemaphoreType.DMA(())]),
        out_shape=jax.ShapeDtypeStruct((bsz, seq, d), F32),
        compiler_params=pltpu.CompilerParams(dimension_semantics=("arbitrary", "arbitrary")),
        name="moe_combine_final",
    )(d1, d2, h, route, nw, yb)


def _moe_slots(route, tm):
    rows = route.shape[0]
    n_assign = 2 * rows
    e = route[:, :2].astype(jnp.int32)
    flat_e = e.reshape(-1)
    onehot = (flat_e[:, None] == jnp.arange(N_EXP, dtype=jnp.int32)[None, :]).astype(jnp.int32)
    csum = jnp.cumsum(onehot, axis=0)
    rank = jnp.sum(csum * onehot, axis=1) - 1
    counts = csum[-1]
    padded = ((counts + tm - 1) // tm) * tm
    ends_p = jnp.cumsum(padded)
    start_p = ends_p - padded
    dest = jnp.sum(start_p[None, :] * onehot, axis=1) + rank
    nblk = n_assign // tm + N_EXP
    cap = nblk * tm
    buf_tok = jnp.zeros((cap,), jnp.int32).at[dest].set(jnp.arange(n_assign, dtype=jnp.int32) // 2)
    blk_start = jnp.arange(nblk, dtype=jnp.int32) * tm
    blk_exp = jnp.minimum(jnp.sum((ends_p[None, :] <= blk_start[:, None]).astype(jnp.int32), axis=-1),
                          N_EXP - 1).astype(jnp.int32)
    n_used = (ends_p[-1:] // tm).astype(jnp.int32)
    dest2 = dest.reshape(rows, 2)
    return buf_tok, blk_exp, n_used, dest2[:, 0], dest2[:, 1], cap


def _lane_row(vals, offset):
    return jnp.zeros((1, CHUNK), F32).at[0, offset:offset + vals.shape[0]].set(vals.astype(F32))


def kernel(x, meta_tokens, norm_mix, w_in, dn_conv_w, dn_a_log, dn_dt_bias, dn_norm_w, df_lambda, df_norm_w, sw_sinks, ssd_conv_w, ssd_conv_b, ssd_a_log, ssd_dt_bias, ssd_d, ssd_norm_w, w_out, norm_ffn, ffn_w_gate, ffn_w_up, ffn_w_down, moe_router, moe_w_gate, moe_w_up, moe_w_down, norm_final):
    bsz, seq, d = x.shape
    depth = w_in.shape[0]
    assert seq % CHUNK == 0 and depth == 2
    lp = PAD + N_META + seq
    nc = lp // CHUNK
    rows = bsz * lp

    meta = jnp.broadcast_to(meta_tokens[None].astype(F32), (bsz, N_META, d))
    h = jnp.concatenate([jnp.zeros((bsz, PAD, d), F32), meta, x.astype(F32)], axis=1).reshape(rows, d)

    y = None
    for l in range(depth):
        wl = w_in[l]
        w_cat = jnp.concatenate(
            [wl[:, 0:1024], wl[:, 1024:1032], wl[:, 3336:3340], jnp.zeros((d, CHUNK - 12), wl.dtype),
             wl[:, 1032:2312], wl[:, 2312:3336]], axis=1).astype(BF16)
        (dn_qkv, dn_z, small, df_q, df_k, df_v, sw_q, sw_kv, ssd_z, ssd_xbc) = _inproj(
            h, norm_mix[l][None].astype(F32), w_cat)

        y_dn = _deltanet(dn_qkv, dn_z, small, dn_conv_w[l].astype(F32),
                         _lane_row(dn_a_log[l], SM_DN_A), _lane_row(dn_dt_bias[l], SM_DN_A),
                         dn_norm_w[l][None].astype(F32), bsz, nc)
        lambda_init = 0.8 - 0.6 * math.exp(-0.3 * l)
        y_df = _diff_attention(df_q, df_k, df_v, df_lambda[l].astype(F32),
                               df_norm_w[l][None].astype(F32), lambda_init, bsz, nc)
        y_sw = _swa(sw_q, sw_kv, sw_sinks[l].astype(F32), bsz, nc)
        y_ssd = _ssd(ssd_xbc, ssd_z, small, ssd_conv_w[l].astype(F32), ssd_conv_b[l][None].astype(F32),
                     _lane_row(ssd_a_log[l], SM_SSD_DT), _lane_row(ssd_dt_bias[l], SM_SSD_DT),
                     ssd_d[l].astype(F32), ssd_norm_w[l][None].astype(F32), bsz, nc)
        h = _outproj(h, (y_dn, y_df, y_sw, y_ssd), w_out[l].astype(BF16))

        i = l // 2
        if l % 2 == 0:
            h = _ffn(h, norm_ffn[l][None].astype(F32), ffn_w_gate[i].astype(BF16),
                     ffn_w_up[i].astype(BF16), ffn_w_down[i].astype(BF16))
        else:
            wr = jnp.zeros((d, CHUNK), F32).at[:, :N_EXP].set(moe_router[i].astype(F32))
            u, route = _router(h, norm_ffn[l][None].astype(F32), wr)
            buf_tok, blk_exp, n_used, d1, d2, cap = _moe_slots(route, TM_MOE)
            xb = _gather_rows(buf_tok, u, cap)
            yb = _experts(blk_exp, n_used, xb, moe_w_gate[i].astype(BF16), moe_w_up[i].astype(BF16),
                          moe_w_down[i].astype(BF16))
            y = _combine_final(d1, d2, h, route, norm_final[None].astype(F32), yb, bsz, nc)
    return y
```

```python
import functools
import math

import jax
import jax.numpy as jnp
from jax import lax
from jax.experimental import pallas as pl
from jax.experimental.pallas import tpu as pltpu

F32 = jnp.float32
BF16 = jnp.bfloat16

N_META = 16
CHUNK = 128
PAD = CHUNK - N_META
EPS = 1e-6
NEG = -1e30
CONV_K = 4

DN_H, DN_D = 4, 64
DF_H, DF_DQK, DF_DV = 4, 32, 64
SW_HQ, SW_HKV, SW_D = 4, 2, 64
SSD_H, SSD_P, SSD_N, SSD_G = 4, 64, 128, 2
N_EXP = 8

SM_DN_B, SM_DN_A, SM_SSD_DT = 0, 4, 8

VMEM_LIMIT = 56 * 1024 * 1024

TM_PROJ = 256
TM_OUT = 512
TM_FFN = 512
TF_DENSE = 1408
TM_MOE = 512
TF_MOE = 896
TG_GATHER = 512
TQ = 2 * CHUNK

HI = lax.Precision.HIGHEST


def _dot(a, b):
    return jnp.dot(a.astype(BF16), b.astype(BF16), preferred_element_type=F32)


def _dot_nt(a, b):
    return lax.dot_general(a.astype(BF16), b.astype(BF16), (((1,), (1,)), ((), ())),
                           preferred_element_type=F32)


def _dot_tn(a, b):
    return lax.dot_general(a.astype(BF16), b.astype(BF16), (((0,), (0,)), ((), ())),
                           preferred_element_type=F32)


def _dot_hi(a, b):
    return jnp.dot(a, b, precision=HI, preferred_element_type=F32)


def _split(x):
    hi = x.astype(BF16)
    return hi, (x - hi.astype(F32)).astype(BF16)


def _dot_x3(a, b):
    a_hi, a_lo = a
    b_hi, b_lo = b
    n = b_hi.shape[1]
    lhs = jnp.concatenate([a_hi, a_lo], axis=1)
    rhs = jnp.concatenate([jnp.concatenate([b_hi, b_lo], axis=1),
                           jnp.concatenate([b_hi, jnp.zeros_like(b_lo)], axis=1)], axis=0)
    o = jnp.dot(lhs, rhs, preferred_element_type=F32)
    return o[:, :n] + o[:, n:]


def _cumsum_rows(ltri_bf16, x):
    hi, lo = _split(x)
    n = x.shape[1]
    o = jnp.dot(ltri_bf16, jnp.concatenate([hi, lo], axis=1), preferred_element_type=F32)
    return o[:, :n] + o[:, n:]


def _sigmoid(x):
    return 1.0 / (1.0 + jnp.exp(-x))


def _silu(x):
    return x * _sigmoid(x)


def _softplus(x):
    return jnp.maximum(x, 0.0) + jnp.log1p(jnp.exp(-jnp.abs(x)))


def _rms(x, w):
    return x * lax.rsqrt(jnp.mean(x * x, axis=-1, keepdims=True) + EPS) * w


def _iota(shape, dim):
    return lax.broadcasted_iota(jnp.int32, shape, dim)


def _causal_conv(buf, x, cw, first):
    width = x.shape[1]

    @pl.when(first)
    def _():
        buf[0:8, :] = jnp.zeros((8, width), F32)

    buf[8:8 + CHUNK, :] = x
    conv = cw[CONV_K - 1:CONV_K, :] * x
    for j in range(CONV_K - 1):
        lo = 8 - (CONV_K - 1) + j
        conv = conv + cw[j:j + 1, :] * buf[lo:lo + CHUNK, :]
    buf[0:8, :] = buf[CHUNK:CHUNK + 8, :]
    return conv


def _chunk_block(b, c, nx):
    return b * (nx + 2) + jnp.where(c == 0, nx, jnp.where(c > nx, c, c - 1))


def _inproj_kernel(h_ref, nw_ref, w_ref, *out_refs, widths):
    u = _rms(h_ref[...], nw_ref[...])
    p = jnp.dot(u.astype(BF16), w_ref[...], preferred_element_type=F32)
    off = 0
    for o_ref, w in zip(out_refs, widths):
        o_ref[...] = p[:, off:off + w].astype(o_ref.dtype)
        off += w


PROJ_WIDTHS = (768, 256, 128, 256, 256, 256, 256, 256, 256, 768)
PROJ_DTYPES = (F32, F32, F32, BF16, BF16, BF16, BF16, BF16, F32, F32)


def _inproj(h, nw, w):
    rows, d = h.shape
    ncol = w.shape[1]
    tm = TM_PROJ
    return pl.pallas_call(
        functools.partial(_inproj_kernel, widths=PROJ_WIDTHS),
        grid=(rows // tm,),
        in_specs=[pl.BlockSpec((tm, d), lambda i: (i, 0)),
                  pl.BlockSpec((1, d), lambda i: (0, 0)),
                  pl.BlockSpec((d, ncol), lambda i: (0, 0))],
        out_specs=[pl.BlockSpec((tm, wd), lambda i: (i, 0)) for wd in PROJ_WIDTHS],
        out_shape=[jax.ShapeDtypeStruct((rows, wd), dt) for wd, dt in zip(PROJ_WIDTHS, PROJ_DTYPES)],
        compiler_params=pltpu.CompilerParams(dimension_semantics=("arbitrary",),
                                             vmem_limit_bytes=VMEM_LIMIT),
        name="inproj",
    )(h, nw, w)


def _dn_kernel(qkv_ref, z_ref, sm_ref, cw_ref, alog_ref, dtb_ref, nw_ref, o_ref, buf, s_ref, *, nx):
    c = pl.program_id(1)

    @pl.when(c > nx)
    def _():
        o_ref[...] = jnp.zeros_like(o_ref)

    @pl.when(c <= nx)
    def _():
        _dn_chunk(c, qkv_ref, z_ref, sm_ref, cw_ref, alog_ref, dtb_ref, nw_ref, o_ref, buf, s_ref)


def _dn_chunk(c, qkv_ref, z_ref, sm_ref, cw_ref, alog_ref, dtb_ref, nw_ref, o_ref, buf, s_ref):
    first = c == 0

    @pl.when(first)
    def _():
        s_ref[...] = jnp.zeros_like(s_ref)

    qkv = _silu(_causal_conv(buf, qkv_ref[...], cw_ref[...], first))
    hd = DN_H * DN_D

    row1 = _iota((CHUNK, 1), 0)
    valid = jnp.logical_or(row1 >= PAD, c > 0)
    row = _iota((CHUNK, CHUNK), 0)
    col = _iota((CHUNK, CHUNK), 1)
    causal = col <= row
    strict = col < row
    eye = (col == row).astype(F32)
    rb16, cb16 = row >> 4, col >> 4
    blk16 = rb16 == cb16
    joins = [jnp.logical_and((rb16 >> (j + 1)) == (cb16 >> (j + 1)), (rb16 >> j) != (cb16 >> j))
             for j in range(3)]

    sm = sm_ref[...]
    lane = _iota((1, CHUNK), 1)
    glane = jnp.logical_and(lane >= SM_DN_A, lane < SM_DN_A + DN_H)
    g_tile = -jnp.exp(alog_ref[...]) * _softplus(sm + dtb_ref[...])
    g_tile = jnp.where(jnp.logical_and(glane, valid), g_tile, 0.0)
    gc_all = _cumsum_rows(causal.astype(BF16), g_tile)
    gc_all_t = gc_all.T

    z = z_ref[...]
    nw = nw_ref[...]
    outs = []
    for h in range(DN_H):
        lo = h * DN_D
        q = qkv[:, lo:lo + DN_D]
        k = qkv[:, hd + lo:hd + lo + DN_D]
        v = qkv[:, 2 * hd + lo:2 * hd + lo + DN_D]
        q = q * lax.rsqrt(jnp.sum(q * q, axis=-1, keepdims=True) + EPS) * (DN_D ** -0.5)
        k = k * lax.rsqrt(jnp.sum(k * k, axis=-1, keepdims=True) + EPS)
        beta = _sigmoid(sm[:, SM_DN_B + h:SM_DN_B + h + 1])
        gl = SM_DN_A + h
        gc = gc_all[:, gl:gl + 1]
        gc_row = gc_all_t[gl:gl + 1, :]
        g_last = gc_all[CHUNK - 1:CHUNK, gl:gl + 1]
        decay = jnp.exp(jnp.where(causal, gc - gc_row, NEG))
        kb = k * beta
        a = jnp.where(strict, _dot_nt(kb, k) * decay, 0.0)
        egc = jnp.exp(gc)
        rhs = jnp.concatenate([v * beta, kb * egc], axis=-1)
        p1 = jnp.where(blk16, -a, 0.0)
        s1 = _split(p1)
        s2 = _split(_dot_x3(s1, s1))
        p4 = _dot_x3(s2, s2)
        s4 = _split(p4)
        s8 = _split(_dot_x3(s4, s4))
        t = eye + p1
        for sp in (s2, s4, s8):
            t = t + _dot_x3(_split(t), sp)
        for jm in joins:
            st = _split(t)
            te = _dot_x3(st, _split(jnp.where(jm, a, 0.0)))
            t = t - _dot_x3(_split(te), st)
        sol = _dot_x3(_split(t), _split(rhs))
        u = sol[:, :DN_D]
        w = sol[:, DN_D:]
        attn = _dot_nt(q, k) * decay
        s_old = s_ref[h]
        v_new = u - _dot(w, s_old)
        o = _dot(q * egc, s_old) + _dot(attn, v_new)
        kdec = k * jnp.exp(g_last - gc)
        s_ref[h] = s_old * jnp.exp(g_last) + _dot_tn(kdec, v_new)
        o = _rms(o, nw) * _silu(z[:, lo:lo + DN_D])
        outs.append(jnp.where(valid, o, 0.0))
    o_ref[...] = jnp.concatenate(outs, axis=-1).astype(o_ref.dtype)


def _deltanet(qkv, z, sm, cw, alog_row, dtb_row, nw, bsz, nx):
    rows = qkv.shape[0]
    rb = lambda b, c: (_chunk_block(b, c, nx), 0)
    fixed = lambda b, c: (0, 0)
    return pl.pallas_call(
        functools.partial(_dn_kernel, nx=nx),
        grid=(bsz, nx + 2),
        in_specs=[pl.BlockSpec((CHUNK, 3 * DN_H * DN_D), rb),
                  pl.BlockSpec((CHUNK, DN_H * DN_D), rb),
                  pl.BlockSpec((CHUNK, CHUNK), rb),
                  pl.BlockSpec((CONV_K, 3 * DN_H * DN_D), fixed),
                  pl.BlockSpec((1, CHUNK), fixed),
                  pl.BlockSpec((1, CHUNK), fixed),
                  pl.BlockSpec((1, DN_D), fixed)],
        out_specs=pl.BlockSpec((CHUNK, DN_H * DN_D), rb),
        out_shape=jax.ShapeDtypeStruct((rows, DN_H * DN_D), BF16),
        scratch_shapes=[pltpu.VMEM((CHUNK + 8, 3 * DN_H * DN_D), F32),
                        pltpu.VMEM((DN_H, DN_D, DN_D), F32)],
        compiler_params=pltpu.CompilerParams(dimension_semantics=("arbitrary", "arbitrary")),
        name="deltanet",
    )(qkv, z, sm, cw, alog_row, dtb_row, nw)


def _df_kernel(q_ref, k_ref, v_ref, lam_ref, nw_ref, o_ref, vt_scr, qm_scr, s_scr, p_scr, m_scr, l_scr,
               acc_scr, *, lambda_init, nx):
    i = pl.program_id(1)
    nt = nx * CHUNK // TQ
    xrows = nx * CHUNK
    nhm = 2 * DF_H
    width = nhm * DF_DQK

    @pl.when(i == 0)
    def _():
        for t in range(nt + 1):
            vt_scr[t] = v_ref[t * TQ:(t + 1) * TQ, :].astype(F32).T.astype(BF16)

    qt = q_ref[...].astype(F32).T * (DF_DQK ** -0.5 * math.log2(math.e))
    frow = _iota((width, TQ), 0)
    for hm in range(nhm):
        sel = jnp.logical_and(frow >= hm * DF_DQK, frow < (hm + 1) * DF_DQK)
        qm_scr[hm] = jnp.where(sel, qt, 0.0).astype(BF16)
    m_scr[...] = jnp.full(m_scr.shape, NEG, F32)
    l_scr[...] = jnp.zeros(l_scr.shape, F32)
    acc_scr[...] = jnp.zeros(acc_scr.shape, F32)

    def tile(kt, vt_of, mask):
        tk = kt.shape[0]
        for hm in range(nhm):
            s = jnp.dot(kt, qm_scr[hm], preferred_element_type=F32)
            if mask is not None:
                s = jnp.where(mask, s, NEG)
            s_scr[hm, 0:tk, :] = s
        for hm in range(nhm):
            m_old = m_scr[hm]
            m_new = jnp.maximum(m_old, jnp.max(s_scr[hm, 0:tk, :], axis=0, keepdims=True))
            m_scr[hm] = m_new
            alpha = jnp.exp2(m_old - m_new)
            p = jnp.exp2(s_scr[hm, 0:tk, :] - m_new)
            l_scr[hm] = alpha * l_scr[hm] + jnp.sum(p, axis=0, keepdims=True)
            p_scr[hm, 0:tk, :] = p.astype(BF16)
            acc_scr[hm] = alpha * acc_scr[hm]
        for hm in range(nhm):
            acc_scr[hm] += jnp.dot(vt_of(hm // 2), p_scr[hm, 0:tk, :], preferred_element_type=F32)

    is_x = i < nt
    krow = _iota((CHUNK, TQ), 0)
    qcol = _iota((CHUNK, TQ), 1)
    head_mask = jnp.logical_and(krow >= PAD, jnp.logical_or(is_x, krow <= qcol))
    tile(k_ref[xrows:xrows + CHUNK, :],
         lambda h: vt_scr[nt, h * DF_DV:(h + 1) * DF_DV, 0:CHUNK], head_mask)

    def body(j, carry):
        off = pl.multiple_of(j * TQ, TQ)
        tile(k_ref[pl.ds(off, TQ), :], lambda h: vt_scr[j, h * DF_DV:(h + 1) * DF_DV, :], None)
        return carry

    lax.fori_loop(0, jnp.where(is_x, i, 0), body, 0)

    @pl.when(is_x)
    def _():
        off = pl.multiple_of(i * TQ, TQ)
        diag = _iota((TQ, TQ), 0) <= _iota((TQ, TQ), 1)
        tile(k_ref[pl.ds(off, TQ), :], lambda h: vt_scr[i, h * DF_DV:(h + 1) * DF_DV, :], diag)

    lp = lam_ref[...]
    lam = (jnp.exp(jnp.sum(lp[0:1] * lp[1:2], axis=-1, keepdims=True))
           - jnp.exp(jnp.sum(lp[2:3] * lp[3:4], axis=-1, keepdims=True)) + lambda_init)
    nw = nw_ref[...]
    outs = []
    for h in range(DF_H):
        o = acc_scr[2 * h] / l_scr[2 * h] - lam * (acc_scr[2 * h + 1] / l_scr[2 * h + 1])
        o = o * lax.rsqrt(jnp.mean(o * o, axis=0, keepdims=True) + EPS) * nw * (1.0 - lambda_init)
        outs.append(o)
    out = jnp.concatenate(outs, axis=0).T
    r = _iota((TQ, 1), 0)
    valid = jnp.logical_or(is_x, jnp.logical_and(r >= PAD, r < CHUNK))
    o_ref[...] = jnp.where(valid, out, 0.0).astype(o_ref.dtype)


def _diff_attention(q, k, v, lam_p, nw_col, lambda_init, bsz, nx):
    rows, width = q.shape
    lps = (nx + 2) * CHUNK
    nt = nx * CHUNK // TQ
    nhm = 2 * DF_H
    return pl.pallas_call(
        functools.partial(_df_kernel, lambda_init=lambda_init, nx=nx),
        grid=(bsz, nt + 1),
        in_specs=[pl.BlockSpec((TQ, width), lambda b, i: (b * (nt + 1) + i, 0)),
                  pl.BlockSpec((lps, width), lambda b, i: (b, 0)),
                  pl.BlockSpec((lps, width), lambda b, i: (b, 0)),
                  pl.BlockSpec((4, DF_DQK), lambda b, i: (0, 0)),
                  pl.BlockSpec((DF_DV, 1), lambda b, i: (0, 0))],
        out_specs=pl.BlockSpec((TQ, width), lambda b, i: (b * (nt + 1) + i, 0)),
        out_shape=jax.ShapeDtypeStruct((rows, width), BF16),
        scratch_shapes=[pltpu.VMEM((nt + 1, width, TQ), BF16),
                        pltpu.VMEM((nhm, width, TQ), BF16),
                        pltpu.VMEM((nhm, TQ, TQ), F32),
                        pltpu.VMEM((nhm, TQ, TQ), BF16),
                        pltpu.VMEM((nhm, 1, TQ), F32),
                        pltpu.VMEM((nhm, 1, TQ), F32),
                        pltpu.VMEM((nhm, DF_DV, TQ), F32)],
        compiler_params=pltpu.CompilerParams(dimension_semantics=("arbitrary", "arbitrary"),
                                             vmem_limit_bytes=VMEM_LIMIT),
        name="diff_attention",
    )(q, k, v, lam_p, nw_col)


def _sw_kernel(sink_ref, q_ref, kv_ref, kvp_ref, o_ref, *, nx):
    c = pl.program_id(1)

    @pl.when(c > nx)
    def _():
        o_ref[...] = jnp.zeros_like(o_ref)

    @pl.when(c <= nx)
    def _():
        _sw_chunk(c, sink_ref, q_ref, kv_ref, kvp_ref, o_ref)


def _sw_chunk(c, sink_ref, q_ref, kv_ref, kvp_ref, o_ref):
    q = q_ref[...].astype(F32) * (SW_D ** -0.5)
    kv = kv_ref[...]
    kvp = kvp_ref[...]
    row = _iota((CHUNK, 2 * CHUNK), 0)
    col = _iota((CHUNK, 2 * CHUNK), 1)
    rel = row - col + CHUNK
    kpos = c * CHUNK - CHUNK + col
    msk = jnp.logical_and(jnp.logical_and(rel >= 0, rel < CHUNK), kpos >= PAD)
    valid = jnp.logical_or(_iota((CHUNK, 1), 0) >= PAD, c > 0)
    kvw = SW_HKV * SW_D
    outs = []
    for g in range(SW_HKV):
        kk = jnp.concatenate([kvp[:, g * SW_D:(g + 1) * SW_D], kv[:, g * SW_D:(g + 1) * SW_D]], axis=0)
        vv = jnp.concatenate([kvp[:, kvw + g * SW_D:kvw + (g + 1) * SW_D],
                              kv[:, kvw + g * SW_D:kvw + (g + 1) * SW_D]], axis=0)
        for r in range(SW_HQ // SW_HKV):
            hq = g * (SW_HQ // SW_HKV) + r
            sink = sink_ref[hq]
            s = jnp.where(msk, _dot_nt(q[:, hq * SW_D:(hq + 1) * SW_D], kk), NEG)
            m = jnp.maximum(jnp.max(s, axis=-1, keepdims=True), sink)
            p = jnp.exp(s - m)
            denom = jnp.sum(p, axis=-1, keepdims=True) + jnp.exp(sink - m)
            o = _dot(p, vv) / denom
            outs.append(jnp.where(valid, o, 0.0))
    o_ref[...] = jnp.concatenate(outs, axis=-1).astype(o_ref.dtype)


def _swa(q, kv, sinks, bsz, nx):
    rows, width = q.shape
    rb = lambda b, c: (_chunk_block(b, c, nx), 0)
    rbp = lambda b, c: (_chunk_block(b, jnp.maximum(c - 1, 0), nx), 0)
    return pl.pallas_call(
        functools.partial(_sw_kernel, nx=nx),
        grid=(bsz, nx + 2),
        in_specs=[pl.BlockSpec(memory_space=pltpu.SMEM),
                  pl.BlockSpec((CHUNK, width), rb),
                  pl.BlockSpec((CHUNK, width), rb),
                  pl.BlockSpec((CHUNK, width), rbp)],
        out_specs=pl.BlockSpec((CHUNK, width), rb),
        out_shape=jax.ShapeDtypeStruct((rows, width), BF16),
        compiler_params=pltpu.CompilerParams(dimension_semantics=("arbitrary", "arbitrary")),
        name="swa",
    )(sinks, q, kv, kv)


def _ssd_kernel(d_ref, xbc_ref, z_ref, sm_ref, cw_ref, cb_ref, alog_ref, dtb_ref, nw_ref, o_ref,
                buf, h_ref, *, nx):
    c = pl.program_id(1)

    @pl.when(c > nx)
    def _():
        o_ref[...] = jnp.zeros_like(o_ref)

    @pl.when(c <= nx)
    def _():
        _ssd_chunk(c, d_ref, xbc_ref, z_ref, sm_ref, cw_ref, cb_ref, alog_ref, dtb_ref, nw_ref, o_ref,
                   buf, h_ref)


def _ssd_chunk(c, d_ref, xbc_ref, z_ref, sm_ref, cw_ref, cb_ref, alog_ref, dtb_ref, nw_ref, o_ref,
               buf, h_ref):
    first = c == 0

    @pl.when(first)
    def _():
        h_ref[...] = jnp.zeros_like(h_ref)

    valid = jnp.logical_or(_iota((CHUNK, 1), 0) >= PAD, c > 0)
    conv = _causal_conv(buf, xbc_ref[...], cw_ref[...], first) + cb_ref[...]
    xbc = jnp.where(valid, _silu(conv), 0.0)
    d_ssm = SSD_H * SSD_P
    gw = SSD_G * SSD_N
    x = xbc[:, :d_ssm]
    bm = xbc[:, d_ssm:d_ssm + gw]
    cm = xbc[:, d_ssm + gw:]

    row = _iota((CHUNK, CHUNK), 0)
    col = _iota((CHUNK, CHUNK), 1)
    causal = col <= row
    lane = _iota((1, CHUNK), 1)
    dlane = jnp.logical_and(lane >= SM_SSD_DT, lane < SM_SSD_DT + SSD_H)
    dt_tile = jnp.where(jnp.logical_and(dlane, valid), _softplus(sm_ref[...] + dtb_ref[...]), 0.0)
    da_tile = dt_tile * (-jnp.exp(alog_ref[...]))
    ac_all = _cumsum_rows(causal.astype(BF16), da_tile)
    ac_all_t = ac_all.T

    hp = SSD_H // SSD_G
    h_old = h_ref[...]
    ys, hs = [], []
    for g in range(SSD_G):
        bg = bm[:, g * SSD_N:(g + 1) * SSD_N]
        cg = cm[:, g * SSD_N:(g + 1) * SSD_N]
        cb = _dot_nt(cg, bg)
        for r in range(hp):
            h = g * hp + r
            al = SM_SSD_DT + h
            dt = dt_tile[:, al:al + 1]
            ac = ac_all[:, al:al + 1]
            ac_row = ac_all_t[al:al + 1, :]
            a_last = ac_all[CHUNK - 1:CHUNK, al:al + 1]
            xh = x[:, h * SSD_P:(h + 1) * SSD_P]
            xdt = xh * dt
            lmat = jnp.exp(jnp.where(causal, ac - ac_row, NEG))
            hprev = h_old[:, h * SSD_P:(h + 1) * SSD_P]
            y = _dot(cb * lmat, xdt) + _dot(cg, hprev) * jnp.exp(ac) + d_ref[h] * xh
            ys.append(y)
            xdec = xdt * jnp.exp(a_last - ac)
            hs.append(hprev * jnp.exp(a_last) + _dot_tn(bg, xdec))
    h_ref[...] = jnp.concatenate(hs, axis=-1)
    y = jnp.concatenate(ys, axis=-1) * _silu(z_ref[...])
    nw = nw_ref[...]
    gwid = d_ssm // SSD_G
    outs = [_rms(y[:, g * gwid:(g + 1) * gwid], nw[:, g * gwid:(g + 1) * gwid]) for g in range(SSD_G)]
    o_ref[...] = jnp.where(valid, jnp.concatenate(outs, axis=-1), 0.0).astype(o_ref.dtype)


def _ssd(xbc, z, sm, cw, cb, alog_row, dtb_row, d_skip, nw, bsz, nx):
    rows, cwid = xbc.shape
    d_ssm = SSD_H * SSD_P
    rb = lambda b, c: (_chunk_block(b, c, nx), 0)
    fixed = lambda b, c: (0, 0)
    return pl.pallas_call(
        functools.partial(_ssd_kernel, nx=nx),
        grid=(bsz, nx + 2),
        in_specs=[pl.BlockSpec(memory_space=pltpu.SMEM),
                  pl.BlockSpec((CHUNK, cwid), rb),
                  pl.BlockSpec((CHUNK, d_ssm), rb),
                  pl.BlockSpec((CHUNK, CHUNK), rb),
                  pl.BlockSpec((CONV_K, cwid), fixed),
                  pl.BlockSpec((1, cwid), fixed),
                  pl.BlockSpec((1, CHUNK), fixed),
                  pl.BlockSpec((1, CHUNK), fixed),
                  pl.BlockSpec((1, d_ssm), fixed)],
        out_specs=pl.BlockSpec((CHUNK, d_ssm), rb),
        out_shape=jax.ShapeDtypeStruct((rows, d_ssm), BF16),
        scratch_shapes=[pltpu.VMEM((CHUNK + 8, cwid), F32),
                        pltpu.VMEM((SSD_N, d_ssm), F32)],
        compiler_params=pltpu.CompilerParams(dimension_semantics=("arbitrary", "arbitrary")),
        name="ssd",
    )(d_skip, xbc, z, sm, cw, cb, alog_row, dtb_row, nw)


def _outproj_kernel(h_ref, y0_ref, y1_ref, y2_ref, y3_ref, w_ref, o_ref):
    acc = h_ref[...]
    for i, y_ref in enumerate((y0_ref, y1_ref, y2_ref, y3_ref)):
        wd = y_ref.shape[1]
        acc = acc + jnp.dot(y_ref[...].astype(BF16), w_ref[i * wd:(i + 1) * wd, :],
                            preferred_element_type=F32)
    o_ref[...] = acc


def _outproj(h, ys, w):
    rows, d = h.shape
    tm = TM_OUT
    wd = ys[0].shape[1]
    return pl.pallas_call(
        _outproj_kernel,
        grid=(rows // tm,),
        in_specs=[pl.BlockSpec((tm, d), lambda i: (i, 0))]
                 + [pl.BlockSpec((tm, wd), lambda i: (i, 0)) for _ in ys]
                 + [pl.BlockSpec(w.shape, lambda i: (0, 0))],
        out_specs=pl.BlockSpec((tm, d), lambda i: (i, 0)),
        out_shape=jax.ShapeDtypeStruct((rows, d), F32),
        compiler_params=pltpu.CompilerParams(dimension_semantics=("arbitrary",),
                                             vmem_limit_bytes=VMEM_LIMIT),
        name="outproj",
    )(h, *ys, w)


def _ffn_kernel(h_ref, nw_ref, wg_ref, wu_ref, wd_ref, o_ref, u_scr, acc_scr):
    f = pl.program_id(1)

    @pl.when(f == 0)
    def _():
        u_scr[...] = _rms(h_ref[...], nw_ref[...]).astype(BF16)
        acc_scr[...] = jnp.zeros_like(acc_scr)

    u = u_scr[...]
    g = jnp.dot(u, wg_ref[...], preferred_element_type=F32)
    up = jnp.dot(u, wu_ref[...], preferred_element_type=F32)
    a = (_silu(g) * up).astype(BF16)
    acc_scr[...] += jnp.dot(a, wd_ref[...], preferred_element_type=F32)

    @pl.when(f == pl.num_programs(1) - 1)
    def _():
        o_ref[...] = h_ref[...] + acc_scr[...]


def _ffn(h, nw, wg, wu, wd):
    rows, d = h.shape
    dff = wg.shape[1]
    tm, tf = TM_FFN, TF_DENSE
    return pl.pallas_call(
        _ffn_kernel,
        grid=(rows // tm, dff // tf),
        in_specs=[pl.BlockSpec((tm, d), lambda i, f: (i, 0)),
                  pl.BlockSpec((1, d), lambda i, f: (0, 0)),
                  pl.BlockSpec((d, tf), lambda i, f: (0, f)),
                  pl.BlockSpec((d, tf), lambda i, f: (0, f)),
                  pl.BlockSpec((tf, d), lambda i, f: (f, 0))],
        out_specs=pl.BlockSpec((tm, d), lambda i, f: (i, 0)),
        out_shape=jax.ShapeDtypeStruct((rows, d), F32),
        scratch_shapes=[pltpu.VMEM((tm, d), BF16), pltpu.VMEM((tm, d), F32)],
        compiler_params=pltpu.CompilerParams(dimension_semantics=("arbitrary", "arbitrary"),
                                             vmem_limit_bytes=VMEM_LIMIT),
        name="ffn_dense",
    )(h, nw, wg, wu, wd)


def _router_kernel(h_ref, nw_ref, wr_ref, u_ref, r_ref):
    u = _rms(h_ref[...], nw_ref[...])
    u_ref[...] = u
    lane = _iota((1, CHUNK), 1)
    logits = jnp.where(lane < N_EXP, _dot_hi(u, wr_ref[...]), -jnp.inf)
    m1 = jnp.max(logits, axis=-1, keepdims=True)
    i1 = jnp.min(jnp.where(logits == m1, lane, CHUNK), axis=-1, keepdims=True)
    rest = jnp.where(lane == i1, -jnp.inf, logits)
    m2 = jnp.max(rest, axis=-1, keepdims=True)
    i2 = jnp.min(jnp.where(rest == m2, lane, CHUNK), axis=-1, keepdims=True)
    e = jnp.exp(m2 - m1)
    g1 = 1.0 / (1.0 + e)
    g2 = e / (1.0 + e)
    out = jnp.where(lane == 0, i1.astype(F32), 0.0)
    out = jnp.where(lane == 1, i2.astype(F32), out)
    out = jnp.where(lane == 2, g1, out)
    out = jnp.where(lane == 3, g2, out)
    r_ref[...] = out


def _router(h, nw, wr):
    rows, d = h.shape
    tm = TM_OUT
    return pl.pallas_call(
        _router_kernel,
        grid=(rows // tm,),
        in_specs=[pl.BlockSpec((tm, d), lambda i: (i, 0)),
                  pl.BlockSpec((1, d), lambda i: (0, 0)),
                  pl.BlockSpec((d, CHUNK), lambda i: (0, 0))],
        out_specs=[pl.BlockSpec((tm, d), lambda i: (i, 0)),
                   pl.BlockSpec((tm, CHUNK), lambda i: (i, 0))],
        out_shape=[jax.ShapeDtypeStruct((rows, d), F32),
                   jax.ShapeDtypeStruct((rows, CHUNK), F32)],
        compiler_params=pltpu.CompilerParams(dimension_semantics=("arbitrary",)),
        name="moe_router",
    )(h, nw, wr)


def _gather_kernel(tok_ref, u_hbm, o_ref, buf, sem):
    tg = buf.shape[0]
    base = pl.program_id(0) * tg

    def issue(r, carry):
        t = tok_ref[base + r]
        pltpu.make_async_copy(u_hbm.at[pl.ds(t, 1), :], buf.at[pl.ds(r, 1), :], sem).start()
        return carry

    lax.fori_loop(0, tg, issue, 0)
    pltpu.make_async_copy(u_hbm.at[pl.ds(0, tg), :], buf, sem).wait()
    o_ref[...] = buf[...].astype(BF16)


def _gather_rows(tok, u, cap):
    d = u.shape[1]
    tg = TG_GATHER
    return pl.pallas_call(
        _gather_kernel,
        grid_spec=pltpu.PrefetchScalarGridSpec(
            num_scalar_prefetch=1,
            grid=(cap // tg,),
            in_specs=[pl.BlockSpec(memory_space=pl.ANY)],
            out_specs=pl.BlockSpec((tg, d), lambda i, tok: (i, 0)),
            scratch_shapes=[pltpu.VMEM((tg, d), F32), pltpu.SemaphoreType.DMA(())]),
        out_shape=jax.ShapeDtypeStruct((cap, d), BF16),
        compiler_params=pltpu.CompilerParams(dimension_semantics=("arbitrary",)),
        name="moe_gather",
    )(tok, u)


def _expert_kernel(be_ref, nu_ref, x_ref, wg_ref, wu_ref, wd_ref, o_ref, acc_scr):
    i = pl.program_id(0)
    f = pl.program_id(1)
    last = f == pl.num_programs(1) - 1
    used = i < nu_ref[0]

    @pl.when(jnp.logical_and(used, f == 0))
    def _():
        acc_scr[...] = jnp.zeros_like(acc_scr)

    @pl.when(used)
    def _():
        x = x_ref[...]
        g = jnp.dot(x, wg_ref[0], preferred_element_type=F32)
        up = jnp.dot(x, wu_ref[0], preferred_element_type=F32)
        a = (_silu(g) * up).astype(BF16)
        acc_scr[...] += jnp.dot(a, wd_ref[0], preferred_element_type=F32)

    @pl.when(jnp.logical_and(used, last))
    def _():
        o_ref[...] = acc_scr[...]

    @pl.when(jnp.logical_and(jnp.logical_not(used), last))
    def _():
        o_ref[...] = jnp.zeros_like(o_ref)


def _experts(blk_exp, n_used, xb, wg, wu, wd):
    cap, d = xb.shape
    dff = wg.shape[2]
    tm, tf = TM_MOE, TF_MOE
    return pl.pallas_call(
        _expert_kernel,
        grid_spec=pltpu.PrefetchScalarGridSpec(
            num_scalar_prefetch=2,
            grid=(cap // tm, dff // tf),
            in_specs=[pl.BlockSpec((tm, d), lambda i, f, be, nu: (i, 0)),
                      pl.BlockSpec((1, d, tf), lambda i, f, be, nu: (be[i], 0, f)),
                      pl.BlockSpec((1, d, tf), lambda i, f, be, nu: (be[i], 0, f)),
                      pl.BlockSpec((1, tf, d), lambda i, f, be, nu: (be[i], f, 0))],
            out_specs=pl.BlockSpec((tm, d), lambda i, f, be, nu: (i, 0)),
            scratch_shapes=[pltpu.VMEM((tm, d), F32)]),
        out_shape=jax.ShapeDtypeStruct((cap, d), F32),
        compiler_params=pltpu.CompilerParams(dimension_semantics=("arbitrary", "arbitrary"),
                                             vmem_limit_bytes=VMEM_LIMIT),
        name="moe_experts",
    )(blk_exp, n_used, xb, wg, wu, wd)


def _combine_kernel(d1_ref, d2_ref, h_ref, r_ref, nw_ref, yb_hbm, o_ref, buf1, buf2, sem, *, nx):
    b = pl.program_id(0)
    c = pl.program_id(1)
    base = (b * (nx + 2) + c) * CHUNK

    def issue(r, carry):
        pltpu.make_async_copy(yb_hbm.at[pl.ds(d1_ref[base + r], 1), :], buf1.at[pl.ds(r, 1), :], sem).start()
        pltpu.make_async_copy(yb_hbm.at[pl.ds(d2_ref[base + r], 1), :], buf2.at[pl.ds(r, 1), :], sem).start()
        return carry

    lax.fori_loop(0, CHUNK, issue, 0)
    pltpu.make_async_copy(yb_hbm.at[pl.ds(0, CHUNK), :], buf1, sem).wait()
    pltpu.make_async_copy(yb_hbm.at[pl.ds(0, CHUNK), :], buf2, sem).wait()
    r = r_ref[...]
    y = h_ref[...] + r[:, 2:3] * buf1[...] + r[:, 3:4] * buf2[...]
    o_ref[0] = _rms(y, nw_ref[...])


def _combine_final(d1, d2, h, route, nw, yb, bsz, nx):
    d = h.shape[1]
    seq = nx * CHUNK
    rb = lambda b, c, d1, d2: (b * (nx + 2) + c, 0)
    return pl.pallas_call(
        functools.partial(_combine_kernel, nx=nx),
        grid_spec=pltpu.PrefetchScalarGridSpec(
            num_scalar_prefetch=2,
            grid=(bsz, nx),
            in_specs=[pl.BlockSpec((CHUNK, d), rb),
                      pl.BlockSpec((CHUNK, CHUNK), rb),
                      pl.BlockSpec((1, d), lambda b, c, d1, d2: (0, 0)),
                      pl.BlockSpec(memory_space=pl.ANY)],
            out_specs=pl.BlockSpec((1, CHUNK, d), lambda b, c, d1, d2: (b, c, 0)),
            scratch_shapes=[pltpu.VMEM((CHUNK, d), F32), pltpu.VMEM((CHUNK, d), F32),
                            pltpu.SemaphoreType.DMA(())]),
        out_shape=jax.ShapeDtypeStruct((bsz, seq, d), F32),
        compiler_params=pltpu.CompilerParams(dimension_semantics=("arbitrary", "arbitrary")),
        name="moe_combine_final",
    )(d1, d2, h, route, nw, yb)


def _moe_slots(route, tm, nx):
    rows = route.shape[0]
    n_assign = 2 * rows
    pos = jnp.arange(rows, dtype=jnp.int32) % ((nx + 2) * CHUNK)
    real = jnp.logical_or(pos < nx * CHUNK,
                          jnp.logical_and(pos >= nx * CHUNK + PAD, pos < (nx + 1) * CHUNK))
    e = jnp.where(real[:, None], route[:, :2].astype(jnp.int32), N_EXP)
    flat_e = e.reshape(-1)
    onehot = (flat_e[:, None] == jnp.arange(N_EXP, dtype=jnp.int32)[None, :]).astype(jnp.int32)
    csum = jnp.cumsum(onehot, axis=0)
    rank = jnp.sum(csum * onehot, axis=1) - 1
    counts = csum[-1]
    padded = ((counts + tm - 1) // tm) * tm
    ends_p = jnp.cumsum(padded)
    start_p = ends_p - padded
    n_real = n_assign // ((nx + 2) * CHUNK) * (nx * CHUNK + N_META)
    nblk = -(-n_real // tm) + N_EXP
    cap = nblk * tm
    dest = jnp.where(flat_e < N_EXP, jnp.sum(start_p[None, :] * onehot, axis=1) + rank, cap)
    buf_tok = jnp.zeros((cap,), jnp.int32).at[dest].set(jnp.arange(n_assign, dtype=jnp.int32) // 2,
                                                        mode="drop")
    blk_start = jnp.arange(nblk, dtype=jnp.int32) * tm
    blk_exp = jnp.minimum(jnp.sum((ends_p[None, :] <= blk_start[:, None]).astype(jnp.int32), axis=-1),
                          N_EXP - 1).astype(jnp.int32)
    n_used = (ends_p[-1:] // tm).astype(jnp.int32)
    dest2 = dest.reshape(rows, 2)
    return buf_tok, blk_exp, n_used, dest2[:, 0], dest2[:, 1], cap


def _lane_row(vals, offset):
    return jnp.zeros((1, CHUNK), F32).at[0, offset:offset + vals.shape[0]].set(vals.astype(F32))


def kernel(x, meta_tokens, norm_mix, w_in, dn_conv_w, dn_a_log, dn_dt_bias, dn_norm_w, df_lambda, df_norm_w, sw_sinks, ssd_conv_w, ssd_conv_b, ssd_a_log, ssd_dt_bias, ssd_d, ssd_norm_w, w_out, norm_ffn, ffn_w_gate, ffn_w_up, ffn_w_down, moe_router, moe_w_gate, moe_w_up, moe_w_down, norm_final):
    bsz, seq, d = x.shape
    depth = w_in.shape[0]
    assert seq % TQ == 0 and depth == 2
    nx = seq // CHUNK
    rows = bsz * (nx + 2) * CHUNK

    meta = jnp.broadcast_to(meta_tokens[None].astype(F32), (bsz, N_META, d))
    h = jnp.concatenate([x.astype(F32), jnp.zeros((bsz, PAD, d), F32), meta,
                         jnp.zeros((bsz, CHUNK, d), F32)], axis=1).reshape(rows, d)

    y = None
    for l in range(depth):
        wl = w_in[l]
        w_cat = jnp.concatenate(
            [wl[:, 0:1024], wl[:, 1024:1032], wl[:, 3336:3340], jnp.zeros((d, CHUNK - 12), wl.dtype),
             wl[:, 1032:2312], wl[:, 2312:3336]], axis=1).astype(BF16)
        (dn_qkv, dn_z, small, df_q, df_k, df_v, sw_q, sw_kv, ssd_z, ssd_xbc) = _inproj(
            h, norm_mix[l][None].astype(F32), w_cat)

        y_dn = _deltanet(dn_qkv, dn_z, small, dn_conv_w[l].astype(F32),
                         _lane_row(dn_a_log[l], SM_DN_A), _lane_row(dn_dt_bias[l], SM_DN_A),
                         dn_norm_w[l][None].astype(F32), bsz, nx)
        lambda_init = 0.8 - 0.6 * math.exp(-0.3 * l)
        y_df = _diff_attention(df_q, df_k, df_v, df_lambda[l].astype(F32),
                               df_norm_w[l][:, None].astype(F32), lambda_init, bsz, nx)
        y_sw = _swa(sw_q, sw_kv, sw_sinks[l].astype(F32), bsz, nx)
        y_ssd = _ssd(ssd_xbc, ssd_z, small, ssd_conv_w[l].astype(F32), ssd_conv_b[l][None].astype(F32),
                     _lane_row(ssd_a_log[l], SM_SSD_DT), _lane_row(ssd_dt_bias[l], SM_SSD_DT),
                     ssd_d[l].astype(F32), ssd_norm_w[l][None].astype(F32), bsz, nx)
        h = _outproj(h, (y_dn, y_df, y_sw, y_ssd), w_out[l].astype(BF16))

        i = l // 2
        if l % 2 == 0:
            h = _ffn(h, norm_ffn[l][None].astype(F32), ffn_w_gate[i].astype(BF16),
                     ffn_w_up[i].astype(BF16), ffn_w_down[i].astype(BF16))
        else:
            wr = jnp.zeros((d, CHUNK), F32).at[:, :N_EXP].set(moe_router[i].astype(F32))
            u, route = _router(h, norm_ffn[l][None].astype(F32), wr)
            buf_tok, blk_exp, n_used, d1, d2, cap = _moe_slots(route, TM_MOE, nx)
            xb = _gather_rows(buf_tok, u, cap)
            yb = _experts(blk_exp, n_used, xb, moe_w_gate[i].astype(BF16), moe_w_up[i].astype(BF16),
                          moe_w_down[i].astype(BF16))
            y = _combine_final(d1, d2, h, route, norm_final[None].astype(F32), yb, bsz, nx)
    return y
```

```python
import functools
import math

import jax
import jax.numpy as jnp
from jax import lax
from jax.experimental import pallas as pl
from jax.experimental.pallas import tpu as pltpu

F32 = jnp.float32
BF16 = jnp.bfloat16

N_META = 16
CHUNK = 128
PAD = CHUNK - N_META
EPS = 1e-6
NEG = -1e30
CONV_K = 4

DN_H, DN_D = 4, 64
DF_H, DF_DQK, DF_DV = 4, 32, 64
SW_HQ, SW_HKV, SW_D = 4, 2, 64
SSD_H, SSD_P, SSD_N, SSD_G = 4, 64, 128, 2
N_EXP = 8

SM_DN_B, SM_DN_A, SM_SSD_DT = 0, 4, 8

VMEM_LIMIT = 56 * 1024 * 1024

TM_PROJ = 256
TM_OUT = 512
TM_FFN = 512
TF_DENSE = 1408
TM_MOE = 512
TF_MOE = 896
TG_GATHER = 512
TQ = 2 * CHUNK

HI = lax.Precision.HIGHEST


def _dot(a, b):
    return jnp.dot(a.astype(BF16), b.astype(BF16), preferred_element_type=F32)


def _dot_nt(a, b):
    return lax.dot_general(a.astype(BF16), b.astype(BF16), (((1,), (1,)), ((), ())),
                           preferred_element_type=F32)


def _dot_tn(a, b):
    return lax.dot_general(a.astype(BF16), b.astype(BF16), (((0,), (0,)), ((), ())),
                           preferred_element_type=F32)


def _dot_hi(a, b):
    return jnp.dot(a, b, precision=HI, preferred_element_type=F32)


def _split(x):
    hi = x.astype(BF16)
    return hi, (x - hi.astype(F32)).astype(BF16)


def _dot_x3(a, b):
    a_hi, a_lo = a
    b_hi, b_lo = b
    n = b_hi.shape[1]
    lhs = jnp.concatenate([a_hi, a_lo], axis=1)
    rhs = jnp.concatenate([jnp.concatenate([b_hi, b_lo], axis=1),
                           jnp.concatenate([b_hi, jnp.zeros_like(b_lo)], axis=1)], axis=0)
    o = jnp.dot(lhs, rhs, preferred_element_type=F32)
    return o[:, :n] + o[:, n:]


def _cumsum_rows(ltri_bf16, x):
    hi, lo = _split(x)
    n = x.shape[1]
    o = jnp.dot(ltri_bf16, jnp.concatenate([hi, lo], axis=1), preferred_element_type=F32)
    return o[:, :n] + o[:, n:]


def _sigmoid(x):
    return 1.0 / (1.0 + jnp.exp(-x))


def _silu(x):
    return x * _sigmoid(x)


def _softplus(x):
    return jnp.maximum(x, 0.0) + jnp.log1p(jnp.exp(-jnp.abs(x)))


def _rms(x, w):
    return x * lax.rsqrt(jnp.mean(x * x, axis=-1, keepdims=True) + EPS) * w


def _iota(shape, dim):
    return lax.broadcasted_iota(jnp.int32, shape, dim)


def _causal_conv(buf, x, cw, first):
    width = x.shape[1]

    @pl.when(first)
    def _():
        buf[0:8, :] = jnp.zeros((8, width), F32)

    buf[8:8 + CHUNK, :] = x
    conv = cw[CONV_K - 1:CONV_K, :] * x
    for j in range(CONV_K - 1):
        lo = 8 - (CONV_K - 1) + j
        conv = conv + cw[j:j + 1, :] * buf[lo:lo + CHUNK, :]
    buf[0:8, :] = buf[CHUNK:CHUNK + 8, :]
    return conv


def _chunk_block(b, c, nx):
    return b * (nx + 2) + jnp.where(c == 0, nx, jnp.where(c > nx, c, c - 1))


def _inproj_kernel(h_ref, nw_ref, w_ref, *out_refs, widths):
    u = _rms(h_ref[...], nw_ref[...])
    p = jnp.dot(u.astype(BF16), w_ref[...], preferred_element_type=F32)
    off = 0
    for o_ref, w in zip(out_refs, widths):
        o_ref[...] = p[:, off:off + w].astype(o_ref.dtype)
        off += w


PROJ_WIDTHS = (768, 256, 128, 256, 256, 256, 256, 256, 256, 768)
PROJ_DTYPES = (F32, F32, F32, BF16, BF16, BF16, BF16, BF16, F32, F32)


def _inproj(h, nw, w):
    rows, d = h.shape
    ncol = w.shape[1]
    tm = TM_PROJ
    return pl.pallas_call(
        functools.partial(_inproj_kernel, widths=PROJ_WIDTHS),
        grid=(rows // tm,),
        in_specs=[pl.BlockSpec((tm, d), lambda i: (i, 0)),
                  pl.BlockSpec((1, d), lambda i: (0, 0)),
                  pl.BlockSpec((d, ncol), lambda i: (0, 0))],
        out_specs=[pl.BlockSpec((tm, wd), lambda i: (i, 0)) for wd in PROJ_WIDTHS],
        out_shape=[jax.ShapeDtypeStruct((rows, wd), dt) for wd, dt in zip(PROJ_WIDTHS, PROJ_DTYPES)],
        compiler_params=pltpu.CompilerParams(dimension_semantics=("arbitrary",),
                                             vmem_limit_bytes=VMEM_LIMIT),
        name="inproj",
    )(h, nw, w)


def _dn_kernel(qkv_ref, z_ref, sm_ref, cw_ref, alog_ref, dtb_ref, nw_ref, o_ref, *scratch, nx):
    c = pl.program_id(1)

    @pl.when(c > nx)
    def _():
        o_ref[...] = jnp.zeros_like(o_ref)

    @pl.when(c <= nx)
    def _():
        _dn_chunk(c, qkv_ref, z_ref, sm_ref, cw_ref, alog_ref, dtb_ref, nw_ref, o_ref, *scratch)


def _put_lhs(ref, i, x):
    hi, lo = _split(x)
    ref[i] = jnp.concatenate([hi, lo], axis=1)


def _put_rhs(ref, i, x):
    hi, lo = _split(x)
    ref[i] = jnp.concatenate([jnp.concatenate([hi, lo], axis=1),
                              jnp.concatenate([hi, jnp.zeros_like(lo)], axis=1)], axis=0)


def _mm_x3(l_ref, li, r_ref, ri):
    o = jnp.dot(l_ref[li], r_ref[ri], preferred_element_type=F32)
    return o[:, :CHUNK] + o[:, CHUNK:]


def _dn_chunk(c, qkv_ref, z_ref, sm_ref, cw_ref, alog_ref, dtb_ref, nw_ref, o_ref, buf, s_ref,
              a_scr, t_scr, lp_scr, rp_scr, lt_scr, rt_scr, rr_scr, attn_scr, qg_scr, k_scr):
    first = c == 0

    @pl.when(first)
    def _():
        s_ref[...] = jnp.zeros_like(s_ref)

    qkv = _silu(_causal_conv(buf, qkv_ref[...], cw_ref[...], first))
    hd = DN_H * DN_D

    row1 = _iota((CHUNK, 1), 0)
    valid = jnp.logical_or(row1 >= PAD, c > 0)
    row = _iota((CHUNK, CHUNK), 0)
    col = _iota((CHUNK, CHUNK), 1)
    causal = col <= row
    strict = col < row
    eye = (col == row).astype(F32)
    rb16, cb16 = row >> 4, col >> 4
    blk16 = rb16 == cb16
    joins = [jnp.logical_and((rb16 >> (j + 1)) == (cb16 >> (j + 1)), (rb16 >> j) != (cb16 >> j))
             for j in range(3)]

    sm = sm_ref[...]
    lane = _iota((1, CHUNK), 1)
    glane = jnp.logical_and(lane >= SM_DN_A, lane < SM_DN_A + DN_H)
    g_tile = -jnp.exp(alog_ref[...]) * _softplus(sm + dtb_ref[...])
    g_tile = jnp.where(jnp.logical_and(glane, valid), g_tile, 0.0)
    gc_all = _cumsum_rows(causal.astype(BF16), g_tile)
    gc_all_t = gc_all.T

    z = z_ref[...]
    nw = nw_ref[...]
    outs = []
    heads = range(DN_H)
    for h in heads:
        lo = h * DN_D
        q = qkv[:, lo:lo + DN_D]
        k = qkv[:, hd + lo:hd + lo + DN_D]
        v = qkv[:, 2 * hd + lo:2 * hd + lo + DN_D]
        q = q * lax.rsqrt(jnp.sum(q * q, axis=-1, keepdims=True) + EPS) * (DN_D ** -0.5)
        k = k * lax.rsqrt(jnp.sum(k * k, axis=-1, keepdims=True) + EPS)
        beta = _sigmoid(sm[:, SM_DN_B + h:SM_DN_B + h + 1])
        gl = SM_DN_A + h
        gc = gc_all[:, gl:gl + 1]
        gc_row = gc_all_t[gl:gl + 1, :]
        decay = jnp.exp(jnp.where(causal, gc - gc_row, NEG))
        kb = k * beta
        a = jnp.where(strict, _dot_nt(kb, k) * decay, 0.0)
        egc = jnp.exp(gc)
        a_scr[h] = a
        _put_rhs(rr_scr, h, jnp.concatenate([v * beta, kb * egc], axis=-1))
        attn_scr[h] = (_dot_nt(q, k) * decay).astype(BF16)
        qg_scr[h] = q * egc
        k_scr[h] = k
        p1 = jnp.where(blk16, -a, 0.0)
        _put_lhs(lp_scr, h, p1)
        _put_rhs(rp_scr, h, p1)
        t_scr[h] = eye + p1
        _put_lhs(lt_scr, h, eye + p1)
    nh = DN_H
    for h in heads:
        p2 = _mm_x3(lp_scr, h, rp_scr, h)
        _put_lhs(lp_scr, nh + h, p2)
        _put_rhs(rp_scr, nh + h, p2)
    for h in heads:
        t = t_scr[h] + _mm_x3(lt_scr, h, rp_scr, nh + h)
        p4 = _mm_x3(lp_scr, nh + h, rp_scr, nh + h)
        t_scr[nh + h] = t
        _put_lhs(lt_scr, nh + h, t)
        _put_lhs(lp_scr, 2 * nh + h, p4)
        _put_rhs(rp_scr, 2 * nh + h, p4)
    for h in heads:
        t = t_scr[nh + h] + _mm_x3(lt_scr, nh + h, rp_scr, 2 * nh + h)
        p8 = _mm_x3(lp_scr, 2 * nh + h, rp_scr, 2 * nh + h)
        t_scr[2 * nh + h] = t
        _put_lhs(lt_scr, 2 * nh + h, t)
        _put_rhs(rp_scr, 3 * nh + h, p8)
    for h in heads:
        t = t_scr[2 * nh + h] + _mm_x3(lt_scr, 2 * nh + h, rp_scr, 3 * nh + h)
        t_scr[3 * nh + h] = t
        _put_lhs(lt_scr, 3 * nh + h, t)
        _put_rhs(rt_scr, h, t)
    for j, jm in enumerate(joins):
        src, dst = (3 + j) * nh, (4 + j) * nh
        for h in heads:
            _put_rhs(rp_scr, dst + h, jnp.where(jm, a_scr[h], 0.0))
            _put_lhs(lp_scr, src + h, _mm_x3(lt_scr, src + h, rp_scr, dst + h))
        for h in heads:
            t = t_scr[src + h] - _mm_x3(lp_scr, src + h, rt_scr, j * nh + h)
            t_scr[dst + h] = t
            _put_lhs(lt_scr, dst + h, t)
            if j + 1 < len(joins):
                _put_rhs(rt_scr, (j + 1) * nh + h, t)
    last = (3 + len(joins)) * nh
    for h in heads:
        lo = h * DN_D
        gl = SM_DN_A + h
        gc = gc_all[:, gl:gl + 1]
        g_last = gc_all[CHUNK - 1:CHUNK, gl:gl + 1]
        sol = _mm_x3(lt_scr, last + h, rr_scr, h)
        u = sol[:, :DN_D]
        w = sol[:, DN_D:]
        s_old = s_ref[h]
        v_new = u - _dot(w, s_old)
        o = _dot(qg_scr[h], s_old) + jnp.dot(attn_scr[h], v_new.astype(BF16), preferred_element_type=F32)
        kdec = k_scr[h] * jnp.exp(g_last - gc)
        s_ref[h] = s_old * jnp.exp(g_last) + _dot_tn(kdec, v_new)
        o = _rms(o, nw) * _silu(z[:, lo:lo + DN_D])
        outs.append(jnp.where(valid, o, 0.0))
    o_ref[...] = jnp.concatenate(outs, axis=-1).astype(o_ref.dtype)


def _deltanet(qkv, z, sm, cw, alog_row, dtb_row, nw, bsz, nx):
    rows = qkv.shape[0]
    rb = lambda b, c: (_chunk_block(b, c, nx), 0)
    fixed = lambda b, c: (0, 0)
    return pl.pallas_call(
        functools.partial(_dn_kernel, nx=nx),
        grid=(bsz, nx + 2),
        in_specs=[pl.BlockSpec((CHUNK, 3 * DN_H * DN_D), rb),
                  pl.BlockSpec((CHUNK, DN_H * DN_D), rb),
                  pl.BlockSpec((CHUNK, CHUNK), rb),
                  pl.BlockSpec((CONV_K, 3 * DN_H * DN_D), fixed),
                  pl.BlockSpec((1, CHUNK), fixed),
                  pl.BlockSpec((1, CHUNK), fixed),
                  pl.BlockSpec((1, DN_D), fixed)],
        out_specs=pl.BlockSpec((CHUNK, DN_H * DN_D), rb),
        out_shape=jax.ShapeDtypeStruct((rows, DN_H * DN_D), BF16),
        scratch_shapes=[pltpu.VMEM((CHUNK + 8, 3 * DN_H * DN_D), F32),
                        pltpu.VMEM((DN_H, DN_D, DN_D), F32),
                        pltpu.VMEM((DN_H, CHUNK, CHUNK), F32),
                        pltpu.VMEM((7 * DN_H, CHUNK, CHUNK), F32),
                        pltpu.VMEM((6 * DN_H, CHUNK, 2 * CHUNK), BF16),
                        pltpu.VMEM((7 * DN_H, 2 * CHUNK, 2 * CHUNK), BF16),
                        pltpu.VMEM((7 * DN_H, CHUNK, 2 * CHUNK), BF16),
                        pltpu.VMEM((3 * DN_H, 2 * CHUNK, 2 * CHUNK), BF16),
                        pltpu.VMEM((DN_H, 2 * CHUNK, 2 * CHUNK), BF16),
                        pltpu.VMEM((DN_H, CHUNK, CHUNK), BF16),
                        pltpu.VMEM((DN_H, CHUNK, DN_D), F32),
                        pltpu.VMEM((DN_H, CHUNK, DN_D), F32)],
        compiler_params=pltpu.CompilerParams(dimension_semantics=("arbitrary", "arbitrary"),
                                             vmem_limit_bytes=VMEM_LIMIT),
        name="deltanet",
    )(qkv, z, sm, cw, alog_row, dtb_row, nw)


def _df_kernel(q_ref, k_ref, v_ref, lam_ref, nw_ref, o_ref, vt_scr, qm_scr, s_scr, p_scr, m_scr, l_scr,
               acc_scr, *, lambda_init, nx):
    i = pl.program_id(1)
    nt = nx * CHUNK // TQ
    xrows = nx * CHUNK
    nhm = 2 * DF_H
    width = nhm * DF_DQK

    @pl.when(i == 0)
    def _():
        for t in range(nt + 1):
            vt_scr[t] = v_ref[t * TQ:(t + 1) * TQ, :].astype(F32).T.astype(BF16)

    qt = q_ref[...].astype(F32).T * (DF_DQK ** -0.5 * math.log2(math.e))
    frow = _iota((width, TQ), 0)
    for hm in range(nhm):
        sel = jnp.logical_and(frow >= hm * DF_DQK, frow < (hm + 1) * DF_DQK)
        qm_scr[hm] = jnp.where(sel, qt, 0.0).astype(BF16)
    m_scr[...] = jnp.full(m_scr.shape, NEG, F32)
    l_scr[...] = jnp.zeros(l_scr.shape, F32)
    acc_scr[...] = jnp.zeros(acc_scr.shape, F32)

    def tile(kt, vt_of, mask):
        tk = kt.shape[0]
        for hm in range(nhm):
            s = jnp.dot(kt, qm_scr[hm], preferred_element_type=F32)
            if mask is not None:
                s = jnp.where(mask, s, NEG)
            s_scr[hm, 0:tk, :] = s
        for hm in range(nhm):
            m_old = m_scr[hm]
            m_new = jnp.maximum(m_old, jnp.max(s_scr[hm, 0:tk, :], axis=0, keepdims=True))
            m_scr[hm] = m_new
            alpha = jnp.exp2(m_old - m_new)
            p = jnp.exp2(s_scr[hm, 0:tk, :] - m_new)
            l_scr[hm] = alpha * l_scr[hm] + jnp.sum(p, axis=0, keepdims=True)
            p_scr[hm, 0:tk, :] = p.astype(BF16)
            acc_scr[hm] = alpha * acc_scr[hm]
        for hm in range(nhm):
            acc_scr[hm] += jnp.dot(vt_of(hm // 2), p_scr[hm, 0:tk, :], preferred_element_type=F32)

    is_x = i < nt
    krow = _iota((CHUNK, TQ), 0)
    qcol = _iota((CHUNK, TQ), 1)
    head_mask = jnp.logical_and(krow >= PAD, jnp.logical_or(is_x, krow <= qcol))
    tile(k_ref[xrows:xrows + CHUNK, :],
         lambda h: vt_scr[nt, h * DF_DV:(h + 1) * DF_DV, 0:CHUNK], head_mask)

    def body(j, carry):
        off = pl.multiple_of(j * TQ, TQ)
        tile(k_ref[pl.ds(off, TQ), :], lambda h: vt_scr[j, h * DF_DV:(h + 1) * DF_DV, :], None)
        return carry

    lax.fori_loop(0, jnp.where(is_x, i, 0), body, 0)

    @pl.when(is_x)
    def _():
        off = pl.multiple_of(i * TQ, TQ)
        diag = _iota((TQ, TQ), 0) <= _iota((TQ, TQ), 1)
        tile(k_ref[pl.ds(off, TQ), :], lambda h: vt_scr[i, h * DF_DV:(h + 1) * DF_DV, :], diag)

    lp = lam_ref[...]
    lam = (jnp.exp(jnp.sum(lp[0:1] * lp[1:2], axis=-1, keepdims=True))
           - jnp.exp(jnp.sum(lp[2:3] * lp[3:4], axis=-1, keepdims=True)) + lambda_init)
    nw = nw_ref[...]
    outs = []
    for h in range(DF_H):
        o = acc_scr[2 * h] / l_scr[2 * h] - lam * (acc_scr[2 * h + 1] / l_scr[2 * h + 1])
        o = o * lax.rsqrt(jnp.mean(o * o, axis=0, keepdims=True) + EPS) * nw * (1.0 - lambda_init)
        outs.append(o)
    out = jnp.concatenate(outs, axis=0).T
    r = _iota((TQ, 1), 0)
    valid = jnp.logical_or(is_x, jnp.logical_and(r >= PAD, r < CHUNK))
    o_ref[...] = jnp.where(valid, out, 0.0).astype(o_ref.dtype)


def _diff_attention(q, k, v, lam_p, nw_col, lambda_init, bsz, nx):
    rows, width = q.shape
    lps = (nx + 2) * CHUNK
    nt = nx * CHUNK // TQ
    nhm = 2 * DF_H
    return pl.pallas_call(
        functools.partial(_df_kernel, lambda_init=lambda_init, nx=nx),
        grid=(bsz, nt + 1),
        in_specs=[pl.BlockSpec((TQ, width), lambda b, i: (b * (nt + 1) + i, 0)),
                  pl.BlockSpec((lps, width), lambda b, i: (b, 0)),
                  pl.BlockSpec((lps, width), lambda b, i: (b, 0)),
                  pl.BlockSpec((4, DF_DQK), lambda b, i: (0, 0)),
                  pl.BlockSpec((DF_DV, 1), lambda b, i: (0, 0))],
        out_specs=pl.BlockSpec((TQ, width), lambda b, i: (b * (nt + 1) + i, 0)),
        out_shape=jax.ShapeDtypeStruct((rows, width), BF16),
        scratch_shapes=[pltpu.VMEM((nt + 1, width, TQ), BF16),
                        pltpu.VMEM((nhm, width, TQ), BF16),
                        pltpu.VMEM((nhm, TQ, TQ), F32),
                        pltpu.VMEM((nhm, TQ, TQ), BF16),
                        pltpu.VMEM((nhm, 1, TQ), F32),
                        pltpu.VMEM((nhm, 1, TQ), F32),
                        pltpu.VMEM((nhm, DF_DV, TQ), F32)],
        compiler_params=pltpu.CompilerParams(dimension_semantics=("arbitrary", "arbitrary"),
                                             vmem_limit_bytes=VMEM_LIMIT),
        name="diff_attention",
    )(q, k, v, lam_p, nw_col)


def _sw_kernel(sink_ref, q_ref, kv_ref, kvp_ref, o_ref, *, nx):
    c = pl.program_id(1)

    @pl.when(c > nx)
    def _():
        o_ref[...] = jnp.zeros_like(o_ref)

    @pl.when(c <= nx)
    def _():
        _sw_chunk(c, sink_ref, q_ref, kv_ref, kvp_ref, o_ref)


def _sw_chunk(c, sink_ref, q_ref, kv_ref, kvp_ref, o_ref):
    q = q_ref[...].astype(F32) * (SW_D ** -0.5)
    kv = kv_ref[...]
    kvp = kvp_ref[...]
    row = _iota((CHUNK, 2 * CHUNK), 0)
    col = _iota((CHUNK, 2 * CHUNK), 1)
    rel = row - col + CHUNK
    kpos = c * CHUNK - CHUNK + col
    msk = jnp.logical_and(jnp.logical_and(rel >= 0, rel < CHUNK), kpos >= PAD)
    valid = jnp.logical_or(_iota((CHUNK, 1), 0) >= PAD, c > 0)
    kvw = SW_HKV * SW_D
    outs = []
    for g in range(SW_HKV):
        kk = jnp.concatenate([kvp[:, g * SW_D:(g + 1) * SW_D], kv[:, g * SW_D:(g + 1) * SW_D]], axis=0)
        vv = jnp.concatenate([kvp[:, kvw + g * SW_D:kvw + (g + 1) * SW_D],
                              kv[:, kvw + g * SW_D:kvw + (g + 1) * SW_D]], axis=0)
        for r in range(SW_HQ // SW_HKV):
            hq = g * (SW_HQ // SW_HKV) + r
            sink = sink_ref[hq]
            s = jnp.where(msk, _dot_nt(q[:, hq * SW_D:(hq + 1) * SW_D], kk), NEG)
            m = jnp.maximum(jnp.max(s, axis=-1, keepdims=True), sink)
            p = jnp.exp(s - m)
            denom = jnp.sum(p, axis=-1, keepdims=True) + jnp.exp(sink - m)
            o = _dot(p, vv) / denom
            outs.append(jnp.where(valid, o, 0.0))
    o_ref[...] = jnp.concatenate(outs, axis=-1).astype(o_ref.dtype)


def _swa(q, kv, sinks, bsz, nx):
    rows, width = q.shape
    rb = lambda b, c: (_chunk_block(b, c, nx), 0)
    rbp = lambda b, c: (_chunk_block(b, jnp.maximum(c - 1, 0), nx), 0)
    return pl.pallas_call(
        functools.partial(_sw_kernel, nx=nx),
        grid=(bsz, nx + 2),
        in_specs=[pl.BlockSpec(memory_space=pltpu.SMEM),
                  pl.BlockSpec((CHUNK, width), rb),
                  pl.BlockSpec((CHUNK, width), rb),
                  pl.BlockSpec((CHUNK, width), rbp)],
        out_specs=pl.BlockSpec((CHUNK, width), rb),
        out_shape=jax.ShapeDtypeStruct((rows, width), BF16),
        compiler_params=pltpu.CompilerParams(dimension_semantics=("arbitrary", "arbitrary")),
        name="swa",
    )(sinks, q, kv, kv)


def _ssd_kernel(d_ref, xbc_ref, z_ref, sm_ref, cw_ref, cb_ref, alog_ref, dtb_ref, nw_ref, o_ref,
                buf, h_ref, *, nx):
    c = pl.program_id(1)

    @pl.when(c > nx)
    def _():
        o_ref[...] = jnp.zeros_like(o_ref)

    @pl.when(c <= nx)
    def _():
        _ssd_chunk(c, d_ref, xbc_ref, z_ref, sm_ref, cw_ref, cb_ref, alog_ref, dtb_ref, nw_ref, o_ref,
                   buf, h_ref)


def _ssd_chunk(c, d_ref, xbc_ref, z_ref, sm_ref, cw_ref, cb_ref, alog_ref, dtb_ref, nw_ref, o_ref,
               buf, h_ref):
    first = c == 0

    @pl.when(first)
    def _():
        h_ref[...] = jnp.zeros_like(h_ref)

    valid = jnp.logical_or(_iota((CHUNK, 1), 0) >= PAD, c > 0)
    conv = _causal_conv(buf, xbc_ref[...], cw_ref[...], first) + cb_ref[...]
    xbc = jnp.where(valid, _silu(conv), 0.0)
    d_ssm = SSD_H * SSD_P
    gw = SSD_G * SSD_N
    x = xbc[:, :d_ssm]
    bm = xbc[:, d_ssm:d_ssm + gw]
    cm = xbc[:, d_ssm + gw:]

    row = _iota((CHUNK, CHUNK), 0)
    col = _iota((CHUNK, CHUNK), 1)
    causal = col <= row
    lane = _iota((1, CHUNK), 1)
    dlane = jnp.logical_and(lane >= SM_SSD_DT, lane < SM_SSD_DT + SSD_H)
    dt_tile = jnp.where(jnp.logical_and(dlane, valid), _softplus(sm_ref[...] + dtb_ref[...]), 0.0)
    da_tile = dt_tile * (-jnp.exp(alog_ref[...]))
    ac_all = _cumsum_rows(causal.astype(BF16), da_tile)
    ac_all_t = ac_all.T

    hp = SSD_H // SSD_G
    h_old = h_ref[...]
    ys, hs = [], []
    for g in range(SSD_G):
        bg = bm[:, g * SSD_N:(g + 1) * SSD_N]
        cg = cm[:, g * SSD_N:(g + 1) * SSD_N]
        cb = _dot_nt(cg, bg)
        for r in range(hp):
            h = g * hp + r
            al = SM_SSD_DT + h
            dt = dt_tile[:, al:al + 1]
            ac = ac_all[:, al:al + 1]
            ac_row = ac_all_t[al:al + 1, :]
            a_last = ac_all[CHUNK - 1:CHUNK, al:al + 1]
            xh = x[:, h * SSD_P:(h + 1) * SSD_P]
            xdt = xh * dt
            lmat = jnp.exp(jnp.where(causal, ac - ac_row, NEG))
            hprev = h_old[:, h * SSD_P:(h + 1) * SSD_P]
            y = _dot(cb * lmat, xdt) + _dot(cg, hprev) * jnp.exp(ac) + d_ref[h] * xh
            ys.append(y)
            xdec = xdt * jnp.exp(a_last - ac)
            hs.append(hprev * jnp.exp(a_last) + _dot_tn(bg, xdec))
    h_ref[...] = jnp.concatenate(hs, axis=-1)
    y = jnp.concatenate(ys, axis=-1) * _silu(z_ref[...])
    nw = nw_ref[...]
    gwid = d_ssm // SSD_G
    outs = [_rms(y[:, g * gwid:(g + 1) * gwid], nw[:, g * gwid:(g + 1) * gwid]) for g in range(SSD_G)]
    o_ref[...] = jnp.where(valid, jnp.concatenate(outs, axis=-1), 0.0).astype(o_ref.dtype)


def _ssd(xbc, z, sm, cw, cb, alog_row, dtb_row, d_skip, nw, bsz, nx):
    rows, cwid = xbc.shape
    d_ssm = SSD_H * SSD_P
    rb = lambda b, c: (_chunk_block(b, c, nx), 0)
    fixed = lambda b, c: (0, 0)
    return pl.pallas_call(
        functools.partial(_ssd_kernel, nx=nx),
        grid=(bsz, nx + 2),
        in_specs=[pl.BlockSpec(memory_space=pltpu.SMEM),
                  pl.BlockSpec((CHUNK, cwid), rb),
                  pl.BlockSpec((CHUNK, d_ssm), rb),
                  pl.BlockSpec((CHUNK, CHUNK), rb),
                  pl.BlockSpec((CONV_K, cwid), fixed),
                  pl.BlockSpec((1, cwid), fixed),
                  pl.BlockSpec((1, CHUNK), fixed),
                  pl.BlockSpec((1, CHUNK), fixed),
                  pl.BlockSpec((1, d_ssm), fixed)],
        out_specs=pl.BlockSpec((CHUNK, d_ssm), rb),
        out_shape=jax.ShapeDtypeStruct((rows, d_ssm), BF16),
        scratch_shapes=[pltpu.VMEM((CHUNK + 8, cwid), F32),
                        pltpu.VMEM((SSD_N, d_ssm), F32)],
        compiler_params=pltpu.CompilerParams(dimension_semantics=("arbitrary", "arbitrary")),
        name="ssd",
    )(d_skip, xbc, z, sm, cw, cb, alog_row, dtb_row, nw)


def _outproj_kernel(h_ref, y0_ref, y1_ref, y2_ref, y3_ref, w_ref, o_ref):
    acc = h_ref[...]
    for i, y_ref in enumerate((y0_ref, y1_ref, y2_ref, y3_ref)):
        wd = y_ref.shape[1]
        acc = acc + jnp.dot(y_ref[...].astype(BF16), w_ref[i * wd:(i + 1) * wd, :],
                            preferred_element_type=F32)
    o_ref[...] = acc


def _outproj(h, ys, w):
    rows, d = h.shape
    tm = TM_OUT
    wd = ys[0].shape[1]
    return pl.pallas_call(
        _outproj_kernel,
        grid=(rows // tm,),
        in_specs=[pl.BlockSpec((tm, d), lambda i: (i, 0))]
                 + [pl.BlockSpec((tm, wd), lambda i: (i, 0)) for _ in ys]
                 + [pl.BlockSpec(w.shape, lambda i: (0, 0))],
        out_specs=pl.BlockSpec((tm, d), lambda i: (i, 0)),
        out_shape=jax.ShapeDtypeStruct((rows, d), F32),
        compiler_params=pltpu.CompilerParams(dimension_semantics=("arbitrary",),
                                             vmem_limit_bytes=VMEM_LIMIT),
        name="outproj",
    )(h, *ys, w)


def _ffn_kernel(h_ref, nw_ref, wg_ref, wu_ref, wd_ref, o_ref, u_scr, acc_scr):
    f = pl.program_id(1)

    @pl.when(f == 0)
    def _():
        u_scr[...] = _rms(h_ref[...], nw_ref[...]).astype(BF16)
        acc_scr[...] = jnp.zeros_like(acc_scr)

    u = u_scr[...]
    g = jnp.dot(u, wg_ref[...], preferred_element_type=F32)
    up = jnp.dot(u, wu_ref[...], preferred_element_type=F32)
    a = (_silu(g) * up).astype(BF16)
    acc_scr[...] += jnp.dot(a, wd_ref[...], preferred_element_type=F32)

    @pl.when(f == pl.num_programs(1) - 1)
    def _():
        o_ref[...] = h_ref[...] + acc_scr[...]


def _ffn(h, nw, wg, wu, wd):
    rows, d = h.shape
    dff = wg.shape[1]
    tm, tf = TM_FFN, TF_DENSE
    return pl.pallas_call(
        _ffn_kernel,
        grid=(rows // tm, dff // tf),
        in_specs=[pl.BlockSpec((tm, d), lambda i, f: (i, 0)),
                  pl.BlockSpec((1, d), lambda i, f: (0, 0)),
                  pl.BlockSpec((d, tf), lambda i, f: (0, f)),
                  pl.BlockSpec((d, tf), lambda i, f: (0, f)),
                  pl.BlockSpec((tf, d), lambda i, f: (f, 0))],
        out_specs=pl.BlockSpec((tm, d), lambda i, f: (i, 0)),
        out_shape=jax.ShapeDtypeStruct((rows, d), F32),
        scratch_shapes=[pltpu.VMEM((tm, d), BF16), pltpu.VMEM((tm, d), F32)],
        compiler_params=pltpu.CompilerParams(dimension_semantics=("arbitrary", "arbitrary"),
                                             vmem_limit_bytes=VMEM_LIMIT),
        name="ffn_dense",
    )(h, nw, wg, wu, wd)


def _router_kernel(h_ref, nw_ref, wr_ref, u_ref, r_ref):
    u = _rms(h_ref[...], nw_ref[...])
    u_ref[...] = u
    lane = _iota((1, CHUNK), 1)
    logits = jnp.where(lane < N_EXP, _dot_hi(u, wr_ref[...]), -jnp.inf)
    m1 = jnp.max(logits, axis=-1, keepdims=True)
    i1 = jnp.min(jnp.where(logits == m1, lane, CHUNK), axis=-1, keepdims=True)
    rest = jnp.where(lane == i1, -jnp.inf, logits)
    m2 = jnp.max(rest, axis=-1, keepdims=True)
    i2 = jnp.min(jnp.where(rest == m2, lane, CHUNK), axis=-1, keepdims=True)
    e = jnp.exp(m2 - m1)
    g1 = 1.0 / (1.0 + e)
    g2 = e / (1.0 + e)
    out = jnp.where(lane == 0, i1.astype(F32), 0.0)
    out = jnp.where(lane == 1, i2.astype(F32), out)
    out = jnp.where(lane == 2, g1, out)
    out = jnp.where(lane == 3, g2, out)
    r_ref[...] = out


def _router(h, nw, wr):
    rows, d = h.shape
    tm = TM_OUT
    return pl.pallas_call(
        _router_kernel,
        grid=(rows // tm,),
        in_specs=[pl.BlockSpec((tm, d), lambda i: (i, 0)),
                  pl.BlockSpec((1, d), lambda i: (0, 0)),
                  pl.BlockSpec((d, CHUNK), lambda i: (0, 0))],
        out_specs=[pl.BlockSpec((tm, d), lambda i: (i, 0)),
                   pl.BlockSpec((tm, CHUNK), lambda i: (i, 0))],
        out_shape=[jax.ShapeDtypeStruct((rows, d), F32),
                   jax.ShapeDtypeStruct((rows, CHUNK), F32)],
        compiler_params=pltpu.CompilerParams(dimension_semantics=("arbitrary",)),
        name="moe_router",
    )(h, nw, wr)


def _gather_kernel(tok_ref, u_hbm, o_ref, buf, sem):
    tg = buf.shape[0]
    base = pl.program_id(0) * tg

    def issue(r, carry):
        t = tok_ref[base + r]
        pltpu.make_async_copy(u_hbm.at[pl.ds(t, 1), :], buf.at[pl.ds(r, 1), :], sem).start()
        return carry

    lax.fori_loop(0, tg, issue, 0, unroll=8)
    pltpu.make_async_copy(u_hbm.at[pl.ds(0, tg), :], buf, sem).wait()
    o_ref[...] = buf[...].astype(BF16)


def _gather_rows(tok, u, cap):
    d = u.shape[1]
    tg = TG_GATHER
    return pl.pallas_call(
        _gather_kernel,
        grid_spec=pltpu.PrefetchScalarGridSpec(
            num_scalar_prefetch=1,
            grid=(cap // tg,),
            in_specs=[pl.BlockSpec(memory_space=pl.ANY)],
            out_specs=pl.BlockSpec((tg, d), lambda i, tok: (i, 0)),
            scratch_shapes=[pltpu.VMEM((tg, d), F32), pltpu.SemaphoreType.DMA(())]),
        out_shape=jax.ShapeDtypeStruct((cap, d), BF16),
        compiler_params=pltpu.CompilerParams(dimension_semantics=("arbitrary",)),
        name="moe_gather",
    )(tok, u)


def _expert_kernel(be_ref, nu_ref, x_ref, wg_ref, wu_ref, wd_ref, o_ref, acc_scr):
    i = pl.program_id(0)
    f = pl.program_id(1)
    last = f == pl.num_programs(1) - 1
    used = i < nu_ref[0]

    @pl.when(jnp.logical_and(used, f == 0))
    def _():
        acc_scr[...] = jnp.zeros_like(acc_scr)

    @pl.when(used)
    def _():
        x = x_ref[...]
        g = jnp.dot(x, wg_ref[0], preferred_element_type=F32)
        up = jnp.dot(x, wu_ref[0], preferred_element_type=F32)
        a = (_silu(g) * up).astype(BF16)
        acc_scr[...] += jnp.dot(a, wd_ref[0], preferred_element_type=F32)

    @pl.when(jnp.logical_and(used, last))
    def _():
        o_ref[...] = acc_scr[...]

    @pl.when(jnp.logical_and(jnp.logical_not(used), last))
    def _():
        o_ref[...] = jnp.zeros_like(o_ref)


def _experts(blk_exp, n_used, xb, wg, wu, wd):
    cap, d = xb.shape
    dff = wg.shape[2]
    tm, tf = TM_MOE, TF_MOE
    return pl.pallas_call(
        _expert_kernel,
        grid_spec=pltpu.PrefetchScalarGridSpec(
            num_scalar_prefetch=2,
            grid=(cap // tm, dff // tf),
            in_specs=[pl.BlockSpec((tm, d), lambda i, f, be, nu: (i, 0)),
                      pl.BlockSpec((1, d, tf), lambda i, f, be, nu: (be[i], 0, f)),
                      pl.BlockSpec((1, d, tf), lambda i, f, be, nu: (be[i], 0, f)),
                      pl.BlockSpec((1, tf, d), lambda i, f, be, nu: (be[i], f, 0))],
            out_specs=pl.BlockSpec((tm, d), lambda i, f, be, nu: (i, 0)),
            scratch_shapes=[pltpu.VMEM((tm, d), F32)]),
        out_shape=jax.ShapeDtypeStruct((cap, d), F32),
        compiler_params=pltpu.CompilerParams(dimension_semantics=("arbitrary", "arbitrary"),
                                             vmem_limit_bytes=VMEM_LIMIT),
        name="moe_experts",
    )(blk_exp, n_used, xb, wg, wu, wd)


def _combine_kernel(d1_ref, d2_ref, h_ref, r_ref, nw_ref, yb_hbm, o_ref, buf1, buf2, sem, *, nx):
    b = pl.program_id(0)
    c = pl.program_id(1)
    base = (b * (nx + 2) + c) * CHUNK

    def issue(r, carry):
        pltpu.make_async_copy(yb_hbm.at[pl.ds(d1_ref[base + r], 1), :], buf1.at[pl.ds(r, 1), :], sem).start()
        pltpu.make_async_copy(yb_hbm.at[pl.ds(d2_ref[base + r], 1), :], buf2.at[pl.ds(r, 1), :], sem).start()
        return carry

    lax.fori_loop(0, CHUNK, issue, 0, unroll=8)
    pltpu.make_async_copy(yb_hbm.at[pl.ds(0, CHUNK), :], buf1, sem).wait()
    pltpu.make_async_copy(yb_hbm.at[pl.ds(0, CHUNK), :], buf2, sem).wait()
    r = r_ref[...]
    y = h_ref[...] + r[:, 2:3] * buf1[...] + r[:, 3:4] * buf2[...]
    o_ref[0] = _rms(y, nw_ref[...])


def _combine_final(d1, d2, h, route, nw, yb, bsz, nx):
    d = h.shape[1]
    seq = nx * CHUNK
    rb = lambda b, c, d1, d2: (b * (nx + 2) + c, 0)
    return pl.pallas_call(
        functools.partial(_combine_kernel, nx=nx),
        grid_spec=pltpu.PrefetchScalarGridSpec(
            num_scalar_prefetch=2,
            grid=(bsz, nx),
            in_specs=[pl.BlockSpec((CHUNK, d), rb),
                      pl.BlockSpec((CHUNK, CHUNK), rb),
                      pl.BlockSpec((1, d), lambda b, c, d1, d2: (0, 0)),
                      pl.BlockSpec(memory_space=pl.ANY)],
            out_specs=pl.BlockSpec((1, CHUNK, d), lambda b, c, d1, d2: (b, c, 0)),
            scratch_shapes=[pltpu.VMEM((CHUNK, d), F32), pltpu.VMEM((CHUNK, d), F32),
                            pltpu.SemaphoreType.DMA(())]),
        out_shape=jax.ShapeDtypeStruct((bsz, seq, d), F32),
        compiler_params=pltpu.CompilerParams(dimension_semantics=("arbitrary", "arbitrary")),
        name="moe_combine_final",
    )(d1, d2, h, route, nw, yb)


def _moe_slots(route, tm, nx):
    rows = route.shape[0]
    n_assign = 2 * rows
    pos = jnp.arange(rows, dtype=jnp.int32) % ((nx + 2) * CHUNK)
    real = jnp.logical_or(pos < nx * CHUNK,
                          jnp.logical_and(pos >= nx * CHUNK + PAD, pos < (nx + 1) * CHUNK))
    e = jnp.where(real[:, None], route[:, :2].astype(jnp.int32), N_EXP)
    flat_e = e.reshape(-1)
    onehot = (flat_e[:, None] == jnp.arange(N_EXP, dtype=jnp.int32)[None, :]).astype(jnp.int32)
    csum = jnp.cumsum(onehot, axis=0)
    rank = jnp.sum(csum * onehot, axis=1) - 1
    counts = csum[-1]
    padded = ((counts + tm - 1) // tm) * tm
    ends_p = jnp.cumsum(padded)
    start_p = ends_p - padded
    n_real = n_assign // ((nx + 2) * CHUNK) * (nx * CHUNK + N_META)
    nblk = -(-n_real // tm) + N_EXP
    cap = nblk * tm
    dest = jnp.where(flat_e < N_EXP, jnp.sum(start_p[None, :] * onehot, axis=1) + rank, cap)
    buf_tok = jnp.zeros((cap,), jnp.int32).at[dest].set(jnp.arange(n_assign, dtype=jnp.int32) // 2,
                                                        mode="drop")
    blk_start = jnp.arange(nblk, dtype=jnp.int32) * tm
    blk_exp = jnp.minimum(jnp.sum((ends_p[None, :] <= blk_start[:, None]).astype(jnp.int32), axis=-1),
                          N_EXP - 1).astype(jnp.int32)
    n_used = (ends_p[-1:] // tm).astype(jnp.int32)
    dest2 = dest.reshape(rows, 2)
    return buf_tok, blk_exp, n_used, dest2[:, 0], dest2[:, 1], cap


def _lane_row(vals, offset):
    return jnp.zeros((1, CHUNK), F32).at[0, offset:offset + vals.shape[0]].set(vals.astype(F32))


def kernel(x, meta_tokens, norm_mix, w_in, dn_conv_w, dn_a_log, dn_dt_bias, dn_norm_w, df_lambda, df_norm_w, sw_sinks, ssd_conv_w, ssd_conv_b, ssd_a_log, ssd_dt_bias, ssd_d, ssd_norm_w, w_out, norm_ffn, ffn_w_gate, ffn_w_up, ffn_w_down, moe_router, moe_w_gate, moe_w_up, moe_w_down, norm_final):
    bsz, seq, d = x.shape
    depth = w_in.shape[0]
    assert seq % TQ == 0 and depth == 2
    nx = seq // CHUNK
    rows = bsz * (nx + 2) * CHUNK

    meta = jnp.broadcast_to(meta_tokens[None].astype(F32), (bsz, N_META, d))
    h = jnp.concatenate([x.astype(F32), jnp.zeros((bsz, PAD, d), F32), meta,
                         jnp.zeros((bsz, CHUNK, d), F32)], axis=1).reshape(rows, d)

    y = None
    for l in range(depth):
        wl = w_in[l]
        w_cat = jnp.concatenate(
            [wl[:, 0:1024], wl[:, 1024:1032], wl[:, 3336:3340], jnp.zeros((d, CHUNK - 12), wl.dtype),
             wl[:, 1032:2312], wl[:, 2312:3336]], axis=1).astype(BF16)
        (dn_qkv, dn_z, small, df_q, df_k, df_v, sw_q, sw_kv, ssd_z, ssd_xbc) = _inproj(
            h, norm_mix[l][None].astype(F32), w_cat)

        y_dn = _deltanet(dn_qkv, dn_z, small, dn_conv_w[l].astype(F32),
                         _lane_row(dn_a_log[l], SM_DN_A), _lane_row(dn_dt_bias[l], SM_DN_A),
                         dn_norm_w[l][None].astype(F32), bsz, nx)
        lambda_init = 0.8 - 0.6 * math.exp(-0.3 * l)
        y_df = _diff_attention(df_q, df_k, df_v, df_lambda[l].astype(F32),
                               df_norm_w[l][:, None].astype(F32), lambda_init, bsz, nx)
        y_sw = _swa(sw_q, sw_kv, sw_sinks[l].astype(F32), bsz, nx)
        y_ssd = _ssd(ssd_xbc, ssd_z, small, ssd_conv_w[l].astype(F32), ssd_conv_b[l][None].astype(F32),
                     _lane_row(ssd_a_log[l], SM_SSD_DT), _lane_row(ssd_dt_bias[l], SM_SSD_DT),
                     ssd_d[l].astype(F32), ssd_norm_w[l][None].astype(F32), bsz, nx)
        h = _outproj(h, (y_dn, y_df, y_sw, y_ssd), w_out[l].astype(BF16))

        i = l // 2
        if l % 2 == 0:
            h = _ffn(h, norm_ffn[l][None].astype(F32), ffn_w_gate[i].astype(BF16),
                     ffn_w_up[i].astype(BF16), ffn_w_down[i].astype(BF16))
        else:
            wr = jnp.zeros((d, CHUNK), F32).at[:, :N_EXP].set(moe_router[i].astype(F32))
            u, route = _router(h, norm_ffn[l][None].astype(F32), wr)
            buf_tok, blk_exp, n_used, d1, d2, cap = _moe_slots(route, TM_MOE, nx)
            xb = _gather_rows(buf_tok, u, cap)
            yb = _experts(blk_exp, n_used, xb, moe_w_gate[i].astype(BF16), moe_w_up[i].astype(BF16),
                          moe_w_down[i].astype(BF16))
            y = _combine_final(d1, d2, h, route, norm_final[None].astype(F32), yb, bsz, nx)
    return y
```

```python
import functools
import math

import jax
import jax.numpy as jnp
from jax import lax
from jax.experimental import pallas as pl
from jax.experimental.pallas import tpu as pltpu

F32 = jnp.float32
BF16 = jnp.bfloat16

N_META = 16
CHUNK = 128
PAD = CHUNK - N_META
EPS = 1e-6
NEG = -1e30
CONV_K = 4

DN_H, DN_D = 4, 64
DF_H, DF_DQK, DF_DV = 4, 32, 64
SW_HQ, SW_HKV, SW_D = 4, 2, 64
SSD_H, SSD_P, SSD_N, SSD_G = 4, 64, 128, 2
N_EXP = 8

SM_DN_B, SM_DN_A, SM_SSD_DT = 0, 4, 8

VMEM_LIMIT = 56 * 1024 * 1024

TM_PROJ = 256
TM_OUT = 512
TM_FFN = 512
TF_DENSE = 1408
TM_MOE = 512
TF_MOE = 896
TQ = 2 * CHUNK

HI = lax.Precision.HIGHEST


def _dot(a, b):
    return jnp.dot(a.astype(BF16), b.astype(BF16), preferred_element_type=F32)


def _dot_nt(a, b):
    return lax.dot_general(a.astype(BF16), b.astype(BF16), (((1,), (1,)), ((), ())),
                           preferred_element_type=F32)


def _dot_tn(a, b):
    return lax.dot_general(a.astype(BF16), b.astype(BF16), (((0,), (0,)), ((), ())),
                           preferred_element_type=F32)


def _dot_hi(a, b):
    return jnp.dot(a, b, precision=HI, preferred_element_type=F32)


def _split(x):
    hi = x.astype(BF16)
    return hi, (x - hi.astype(F32)).astype(BF16)


def _dot_x3(a, b):
    a_hi, a_lo = a
    b_hi, b_lo = b
    n = b_hi.shape[1]
    lhs = jnp.concatenate([a_hi, a_lo], axis=1)
    rhs = jnp.concatenate([jnp.concatenate([b_hi, b_lo], axis=1),
                           jnp.concatenate([b_hi, jnp.zeros_like(b_lo)], axis=1)], axis=0)
    o = jnp.dot(lhs, rhs, preferred_element_type=F32)
    return o[:, :n] + o[:, n:]


def _cumsum_rows(ltri_bf16, x):
    hi, lo = _split(x)
    n = x.shape[1]
    o = jnp.dot(ltri_bf16, jnp.concatenate([hi, lo], axis=1), preferred_element_type=F32)
    return o[:, :n] + o[:, n:]


def _sigmoid(x):
    return 1.0 / (1.0 + jnp.exp(-x))


def _silu(x):
    return x * _sigmoid(x)


def _softplus(x):
    return jnp.maximum(x, 0.0) + jnp.log1p(jnp.exp(-jnp.abs(x)))


def _rms(x, w):
    return x * lax.rsqrt(jnp.mean(x * x, axis=-1, keepdims=True) + EPS) * w


def _iota(shape, dim):
    return lax.broadcasted_iota(jnp.int32, shape, dim)


def _causal_conv(buf, x, cw, first):
    width = x.shape[1]

    @pl.when(first)
    def _():
        buf[0:8, :] = jnp.zeros((8, width), F32)

    buf[8:8 + CHUNK, :] = x
    conv = cw[CONV_K - 1:CONV_K, :] * x
    for j in range(CONV_K - 1):
        lo = 8 - (CONV_K - 1) + j
        conv = conv + cw[j:j + 1, :] * buf[lo:lo + CHUNK, :]
    buf[0:8, :] = buf[CHUNK:CHUNK + 8, :]
    return conv


def _chunk_block(b, c, nx):
    return b * (nx + 2) + jnp.where(c == 0, nx, jnp.where(c > nx, c, c - 1))


def _inproj_kernel(h_ref, nw_ref, w_ref, *out_refs, widths):
    u = _rms(h_ref[...], nw_ref[...])
    p = jnp.dot(u.astype(BF16), w_ref[...], preferred_element_type=F32)
    off = 0
    for o_ref, w in zip(out_refs, widths):
        o_ref[...] = p[:, off:off + w].astype(o_ref.dtype)
        off += w


PROJ_WIDTHS = (768, 256, 128, 256, 256, 256, 256, 256, 256, 768)
PROJ_DTYPES = (F32, F32, F32, BF16, BF16, BF16, BF16, BF16, F32, F32)


def _inproj(h, nw, w):
    rows, d = h.shape
    ncol = w.shape[1]
    tm = TM_PROJ
    return pl.pallas_call(
        functools.partial(_inproj_kernel, widths=PROJ_WIDTHS),
        grid=(rows // tm,),
        in_specs=[pl.BlockSpec((tm, d), lambda i: (i, 0)),
                  pl.BlockSpec((1, d), lambda i: (0, 0)),
                  pl.BlockSpec((d, ncol), lambda i: (0, 0))],
        out_specs=[pl.BlockSpec((tm, wd), lambda i: (i, 0)) for wd in PROJ_WIDTHS],
        out_shape=[jax.ShapeDtypeStruct((rows, wd), dt) for wd, dt in zip(PROJ_WIDTHS, PROJ_DTYPES)],
        compiler_params=pltpu.CompilerParams(dimension_semantics=("arbitrary",),
                                             vmem_limit_bytes=VMEM_LIMIT),
        name="inproj",
    )(h, nw, w)


def _dn_kernel(qkv_ref, z_ref, sm_ref, cw_ref, alog_ref, dtb_ref, nw_ref, o_ref, *scratch, nx):
    c = pl.program_id(1)

    @pl.when(c > nx)
    def _():
        o_ref[...] = jnp.zeros_like(o_ref)

    @pl.when(c <= nx)
    def _():
        _dn_chunk(c, qkv_ref, z_ref, sm_ref, cw_ref, alog_ref, dtb_ref, nw_ref, o_ref, *scratch)


def _put_lhs(ref, i, x):
    hi, lo = _split(x)
    ref[i] = jnp.concatenate([hi, lo], axis=1)


def _put_rhs(ref, i, x):
    hi, lo = _split(x)
    ref[i] = jnp.concatenate([jnp.concatenate([hi, lo], axis=1),
                              jnp.concatenate([hi, jnp.zeros_like(lo)], axis=1)], axis=0)


def _mm_x3(l_ref, li, r_ref, ri):
    o = jnp.dot(l_ref[li], r_ref[ri], preferred_element_type=F32)
    return o[:, :CHUNK] + o[:, CHUNK:]


def _dn_chunk(c, qkv_ref, z_ref, sm_ref, cw_ref, alog_ref, dtb_ref, nw_ref, o_ref, buf, s_ref,
              a_scr, t_scr, lp_scr, rp_scr, lt_scr, rt_scr, rr_scr, attn_scr, qg_scr, k_scr):
    first = c == 0

    @pl.when(first)
    def _():
        s_ref[...] = jnp.zeros_like(s_ref)

    qkv = _silu(_causal_conv(buf, qkv_ref[...], cw_ref[...], first))
    hd = DN_H * DN_D

    row1 = _iota((CHUNK, 1), 0)
    valid = jnp.logical_or(row1 >= PAD, c > 0)
    row = _iota((CHUNK, CHUNK), 0)
    col = _iota((CHUNK, CHUNK), 1)
    causal = col <= row
    strict = col < row
    eye = (col == row).astype(F32)
    rb16, cb16 = row >> 4, col >> 4
    blk16 = rb16 == cb16
    joins = [jnp.logical_and((rb16 >> (j + 1)) == (cb16 >> (j + 1)), (rb16 >> j) != (cb16 >> j))
             for j in range(3)]

    sm = sm_ref[...]
    lane = _iota((1, CHUNK), 1)
    glane = jnp.logical_and(lane >= SM_DN_A, lane < SM_DN_A + DN_H)
    g_tile = -jnp.exp(alog_ref[...]) * _softplus(sm + dtb_ref[...])
    g_tile = jnp.where(jnp.logical_and(glane, valid), g_tile, 0.0)
    gc_all = _cumsum_rows(causal.astype(BF16), g_tile)
    gc_all_t = gc_all.T

    z = z_ref[...]
    nw = nw_ref[...]
    outs = []
    heads = range(DN_H)
    for h in heads:
        lo = h * DN_D
        q = qkv[:, lo:lo + DN_D]
        k = qkv[:, hd + lo:hd + lo + DN_D]
        v = qkv[:, 2 * hd + lo:2 * hd + lo + DN_D]
        q = q * lax.rsqrt(jnp.sum(q * q, axis=-1, keepdims=True) + EPS) * (DN_D ** -0.5)
        k = k * lax.rsqrt(jnp.sum(k * k, axis=-1, keepdims=True) + EPS)
        beta = _sigmoid(sm[:, SM_DN_B + h:SM_DN_B + h + 1])
        gl = SM_DN_A + h
        gc = gc_all[:, gl:gl + 1]
        gc_row = gc_all_t[gl:gl + 1, :]
        decay = jnp.exp(jnp.where(causal, gc - gc_row, NEG))
        kb = k * beta
        a = jnp.where(strict, _dot_nt(kb, k) * decay, 0.0)
        egc = jnp.exp(gc)
        a_scr[h] = a
        _put_rhs(rr_scr, h, jnp.concatenate([v * beta, kb * egc], axis=-1))
        attn_scr[h] = (_dot_nt(q, k) * decay).astype(BF16)
        qg_scr[h] = q * egc
        k_scr[h] = k
        p1 = jnp.where(blk16, -a, 0.0)
        _put_lhs(lp_scr, h, p1)
        _put_rhs(rp_scr, h, p1)
        t_scr[h] = eye + p1
        _put_lhs(lt_scr, h, eye + p1)
    nh = DN_H
    for h in heads:
        p2 = _mm_x3(lp_scr, h, rp_scr, h)
        _put_lhs(lp_scr, nh + h, p2)
        _put_rhs(rp_scr, nh + h, p2)
    for h in heads:
        t = t_scr[h] + _mm_x3(lt_scr, h, rp_scr, nh + h)
        p4 = _mm_x3(lp_scr, nh + h, rp_scr, nh + h)
        t_scr[nh + h] = t
        _put_lhs(lt_scr, nh + h, t)
        _put_lhs(lp_scr, 2 * nh + h, p4)
        _put_rhs(rp_scr, 2 * nh + h, p4)
    for h in heads:
        t = t_scr[nh + h] + _mm_x3(lt_scr, nh + h, rp_scr, 2 * nh + h)
        p8 = _mm_x3(lp_scr, 2 * nh + h, rp_scr, 2 * nh + h)
        t_scr[2 * nh + h] = t
        _put_lhs(lt_scr, 2 * nh + h, t)
        _put_rhs(rp_scr, 3 * nh + h, p8)
    for h in heads:
        t = t_scr[2 * nh + h] + _mm_x3(lt_scr, 2 * nh + h, rp_scr, 3 * nh + h)
        t_scr[3 * nh + h] = t
        _put_lhs(lt_scr, 3 * nh + h, t)
        _put_rhs(rt_scr, h, t)
    for j, jm in enumerate(joins):
        src, dst = (3 + j) * nh, (4 + j) * nh
        for h in heads:
            _put_rhs(rp_scr, dst + h, jnp.where(jm, a_scr[h], 0.0))
            _put_lhs(lp_scr, src + h, _mm_x3(lt_scr, src + h, rp_scr, dst + h))
        for h in heads:
            t = t_scr[src + h] - _mm_x3(lp_scr, src + h, rt_scr, j * nh + h)
            t_scr[dst + h] = t
            _put_lhs(lt_scr, dst + h, t)
            if j + 1 < len(joins):
                _put_rhs(rt_scr, (j + 1) * nh + h, t)
    last = (3 + len(joins)) * nh
    for h in heads:
        lo = h * DN_D
        gl = SM_DN_A + h
        gc = gc_all[:, gl:gl + 1]
        g_last = gc_all[CHUNK - 1:CHUNK, gl:gl + 1]
        sol = _mm_x3(lt_scr, last + h, rr_scr, h)
        u = sol[:, :DN_D]
        w = sol[:, DN_D:]
        s_old = s_ref[h]
        v_new = u - _dot(w, s_old)
        o = _dot(qg_scr[h], s_old) + jnp.dot(attn_scr[h], v_new.astype(BF16), preferred_element_type=F32)
        kdec = k_scr[h] * jnp.exp(g_last - gc)
        s_ref[h] = s_old * jnp.exp(g_last) + _dot_tn(kdec, v_new)
        o = _rms(o, nw) * _silu(z[:, lo:lo + DN_D])
        outs.append(jnp.where(valid, o, 0.0))
    o_ref[...] = jnp.concatenate(outs, axis=-1).astype(o_ref.dtype)


def _deltanet(qkv, z, sm, cw, alog_row, dtb_row, nw, bsz, nx):
    rows = qkv.shape[0]
    rb = lambda b, c: (_chunk_block(b, c, nx), 0)
    fixed = lambda b, c: (0, 0)
    return pl.pallas_call(
        functools.partial(_dn_kernel, nx=nx),
        grid=(bsz, nx + 2),
        in_specs=[pl.BlockSpec((CHUNK, 3 * DN_H * DN_D), rb),
                  pl.BlockSpec((CHUNK, DN_H * DN_D), rb),
                  pl.BlockSpec((CHUNK, CHUNK), rb),
                  pl.BlockSpec((CONV_K, 3 * DN_H * DN_D), fixed),
                  pl.BlockSpec((1, CHUNK), fixed),
                  pl.BlockSpec((1, CHUNK), fixed),
                  pl.BlockSpec((1, DN_D), fixed)],
        out_specs=pl.BlockSpec((CHUNK, DN_H * DN_D), rb),
        out_shape=jax.ShapeDtypeStruct((rows, DN_H * DN_D), BF16),
        scratch_shapes=[pltpu.VMEM((CHUNK + 8, 3 * DN_H * DN_D), F32),
                        pltpu.VMEM((DN_H, DN_D, DN_D), F32),
                        pltpu.VMEM((DN_H, CHUNK, CHUNK), F32),
                        pltpu.VMEM((7 * DN_H, CHUNK, CHUNK), F32),
                        pltpu.VMEM((6 * DN_H, CHUNK, 2 * CHUNK), BF16),
                        pltpu.VMEM((7 * DN_H, 2 * CHUNK, 2 * CHUNK), BF16),
                        pltpu.VMEM((7 * DN_H, CHUNK, 2 * CHUNK), BF16),
                        pltpu.VMEM((3 * DN_H, 2 * CHUNK, 2 * CHUNK), BF16),
                        pltpu.VMEM((DN_H, 2 * CHUNK, 2 * CHUNK), BF16),
                        pltpu.VMEM((DN_H, CHUNK, CHUNK), BF16),
                        pltpu.VMEM((DN_H, CHUNK, DN_D), F32),
                        pltpu.VMEM((DN_H, CHUNK, DN_D), F32)],
        compiler_params=pltpu.CompilerParams(dimension_semantics=("arbitrary", "arbitrary"),
                                             vmem_limit_bytes=VMEM_LIMIT),
        name="deltanet",
    )(qkv, z, sm, cw, alog_row, dtb_row, nw)


def _df_kernel(q_ref, k_ref, v_ref, lam_ref, nw_ref, o_ref, vt_scr, qm_scr, s_scr, p_scr, m_scr, l_scr,
               acc_scr, *, lambda_init, nx):
    i = pl.program_id(1)
    nt = nx * CHUNK // TQ
    xrows = nx * CHUNK
    nhm = 2 * DF_H
    width = nhm * DF_DQK

    @pl.when(i == 0)
    def _():
        for t in range(nt + 1):
            vt_scr[t] = v_ref[t * TQ:(t + 1) * TQ, :].astype(F32).T.astype(BF16)

    qt = q_ref[...].astype(F32).T * (DF_DQK ** -0.5 * math.log2(math.e))
    frow = _iota((width, TQ), 0)
    for hm in range(nhm):
        sel = jnp.logical_and(frow >= hm * DF_DQK, frow < (hm + 1) * DF_DQK)
        qm_scr[hm] = jnp.where(sel, qt, 0.0).astype(BF16)
    m_scr[...] = jnp.full(m_scr.shape, NEG, F32)
    l_scr[...] = jnp.zeros(l_scr.shape, F32)
    acc_scr[...] = jnp.zeros(acc_scr.shape, F32)

    def tile(kt, vt_of, mask):
        tk = kt.shape[0]
        for hm in range(nhm):
            s = jnp.dot(kt, qm_scr[hm], preferred_element_type=F32)
            if mask is not None:
                s = jnp.where(mask, s, NEG)
            s_scr[hm, 0:tk, :] = s
        for hm in range(nhm):
            m_old = m_scr[hm]
            m_new = jnp.maximum(m_old, jnp.max(s_scr[hm, 0:tk, :], axis=0, keepdims=True))
            m_scr[hm] = m_new
            alpha = jnp.exp2(m_old - m_new)
            p = jnp.exp2(s_scr[hm, 0:tk, :] - m_new)
            l_scr[hm] = alpha * l_scr[hm] + jnp.sum(p, axis=0, keepdims=True)
            p_scr[hm, 0:tk, :] = p.astype(BF16)
            acc_scr[hm] = alpha * acc_scr[hm]
        for hm in range(nhm):
            acc_scr[hm] += jnp.dot(vt_of(hm // 2), p_scr[hm, 0:tk, :], preferred_element_type=F32)

    is_x = i < nt
    krow = _iota((CHUNK, TQ), 0)
    qcol = _iota((CHUNK, TQ), 1)
    head_mask = jnp.logical_and(krow >= PAD, jnp.logical_or(is_x, krow <= qcol))
    tile(k_ref[xrows:xrows + CHUNK, :],
         lambda h: vt_scr[nt, h * DF_DV:(h + 1) * DF_DV, 0:CHUNK], head_mask)

    def body(j, carry):
        off = pl.multiple_of(j * TQ, TQ)
        tile(k_ref[pl.ds(off, TQ), :], lambda h: vt_scr[j, h * DF_DV:(h + 1) * DF_DV, :], None)
        return carry

    lax.fori_loop(0, jnp.where(is_x, i, 0), body, 0)

    @pl.when(is_x)
    def _():
        off = pl.multiple_of(i * TQ, TQ)
        diag = _iota((TQ, TQ), 0) <= _iota((TQ, TQ), 1)
        tile(k_ref[pl.ds(off, TQ), :], lambda h: vt_scr[i, h * DF_DV:(h + 1) * DF_DV, :], diag)

    lp = lam_ref[...]
    lam = (jnp.exp(jnp.sum(lp[0:1] * lp[1:2], axis=-1, keepdims=True))
           - jnp.exp(jnp.sum(lp[2:3] * lp[3:4], axis=-1, keepdims=True)) + lambda_init)
    nw = nw_ref[...]
    outs = []
    for h in range(DF_H):
        o = acc_scr[2 * h] / l_scr[2 * h] - lam * (acc_scr[2 * h + 1] / l_scr[2 * h + 1])
        o = o * lax.rsqrt(jnp.mean(o * o, axis=0, keepdims=True) + EPS) * nw * (1.0 - lambda_init)
        outs.append(o)
    out = jnp.concatenate(outs, axis=0).T
    r = _iota((TQ, 1), 0)
    valid = jnp.logical_or(is_x, jnp.logical_and(r >= PAD, r < CHUNK))
    o_ref[...] = jnp.where(valid, out, 0.0).astype(o_ref.dtype)


def _diff_attention(q, k, v, lam_p, nw_col, lambda_init, bsz, nx):
    rows, width = q.shape
    lps = (nx + 2) * CHUNK
    nt = nx * CHUNK // TQ
    nhm = 2 * DF_H
    return pl.pallas_call(
        functools.partial(_df_kernel, lambda_init=lambda_init, nx=nx),
        grid=(bsz, nt + 1),
        in_specs=[pl.BlockSpec((TQ, width), lambda b, i: (b * (nt + 1) + i, 0)),
                  pl.BlockSpec((lps, width), lambda b, i: (b, 0)),
                  pl.BlockSpec((lps, width), lambda b, i: (b, 0)),
                  pl.BlockSpec((4, DF_DQK), lambda b, i: (0, 0)),
                  pl.BlockSpec((DF_DV, 1), lambda b, i: (0, 0))],
        out_specs=pl.BlockSpec((TQ, width), lambda b, i: (b * (nt + 1) + i, 0)),
        out_shape=jax.ShapeDtypeStruct((rows, width), BF16),
        scratch_shapes=[pltpu.VMEM((nt + 1, width, TQ), BF16),
                        pltpu.VMEM((nhm, width, TQ), BF16),
                        pltpu.VMEM((nhm, TQ, TQ), F32),
                        pltpu.VMEM((nhm, TQ, TQ), BF16),
                        pltpu.VMEM((nhm, 1, TQ), F32),
                        pltpu.VMEM((nhm, 1, TQ), F32),
                        pltpu.VMEM((nhm, DF_DV, TQ), F32)],
        compiler_params=pltpu.CompilerParams(dimension_semantics=("arbitrary", "arbitrary"),
                                             vmem_limit_bytes=VMEM_LIMIT),
        name="diff_attention",
    )(q, k, v, lam_p, nw_col)


def _sw_kernel(sink_ref, q_ref, kv_ref, kvp_ref, o_ref, *, nx):
    c = pl.program_id(1)

    @pl.when(c > nx)
    def _():
        o_ref[...] = jnp.zeros_like(o_ref)

    @pl.when(c <= nx)
    def _():
        _sw_chunk(c, sink_ref, q_ref, kv_ref, kvp_ref, o_ref)


def _sw_chunk(c, sink_ref, q_ref, kv_ref, kvp_ref, o_ref):
    q = q_ref[...].astype(F32) * (SW_D ** -0.5)
    kv = kv_ref[...]
    kvp = kvp_ref[...]
    row = _iota((CHUNK, 2 * CHUNK), 0)
    col = _iota((CHUNK, 2 * CHUNK), 1)
    rel = row - col + CHUNK
    kpos = c * CHUNK - CHUNK + col
    msk = jnp.logical_and(jnp.logical_and(rel >= 0, rel < CHUNK), kpos >= PAD)
    valid = jnp.logical_or(_iota((CHUNK, 1), 0) >= PAD, c > 0)
    kvw = SW_HKV * SW_D
    outs = []
    for g in range(SW_HKV):
        kk = jnp.concatenate([kvp[:, g * SW_D:(g + 1) * SW_D], kv[:, g * SW_D:(g + 1) * SW_D]], axis=0)
        vv = jnp.concatenate([kvp[:, kvw + g * SW_D:kvw + (g + 1) * SW_D],
                              kv[:, kvw + g * SW_D:kvw + (g + 1) * SW_D]], axis=0)
        for r in range(SW_HQ // SW_HKV):
            hq = g * (SW_HQ // SW_HKV) + r
            sink = sink_ref[hq]
            s = jnp.where(msk, _dot_nt(q[:, hq * SW_D:(hq + 1) * SW_D], kk), NEG)
            m = jnp.maximum(jnp.max(s, axis=-1, keepdims=True), sink)
            p = jnp.exp(s - m)
            denom = jnp.sum(p, axis=-1, keepdims=True) + jnp.exp(sink - m)
            o = _dot(p, vv) / denom
            outs.append(jnp.where(valid, o, 0.0))
    o_ref[...] = jnp.concatenate(outs, axis=-1).astype(o_ref.dtype)


def _swa(q, kv, sinks, bsz, nx):
    rows, width = q.shape
    rb = lambda b, c: (_chunk_block(b, c, nx), 0)
    rbp = lambda b, c: (_chunk_block(b, jnp.maximum(c - 1, 0), nx), 0)
    return pl.pallas_call(
        functools.partial(_sw_kernel, nx=nx),
        grid=(bsz, nx + 2),
        in_specs=[pl.BlockSpec(memory_space=pltpu.SMEM),
                  pl.BlockSpec((CHUNK, width), rb),
                  pl.BlockSpec((CHUNK, width), rb),
                  pl.BlockSpec((CHUNK, width), rbp)],
        out_specs=pl.BlockSpec((CHUNK, width), rb),
        out_shape=jax.ShapeDtypeStruct((rows, width), BF16),
        compiler_params=pltpu.CompilerParams(dimension_semantics=("arbitrary", "arbitrary")),
        name="swa",
    )(sinks, q, kv, kv)


def _ssd_kernel(d_ref, xbc_ref, z_ref, sm_ref, cw_ref, cb_ref, alog_ref, dtb_ref, nw_ref, o_ref,
                buf, h_ref, *, nx):
    c = pl.program_id(1)

    @pl.when(c > nx)
    def _():
        o_ref[...] = jnp.zeros_like(o_ref)

    @pl.when(c <= nx)
    def _():
        _ssd_chunk(c, d_ref, xbc_ref, z_ref, sm_ref, cw_ref, cb_ref, alog_ref, dtb_ref, nw_ref, o_ref,
                   buf, h_ref)


def _ssd_chunk(c, d_ref, xbc_ref, z_ref, sm_ref, cw_ref, cb_ref, alog_ref, dtb_ref, nw_ref, o_ref,
               buf, h_ref):
    first = c == 0

    @pl.when(first)
    def _():
        h_ref[...] = jnp.zeros_like(h_ref)

    valid = jnp.logical_or(_iota((CHUNK, 1), 0) >= PAD, c > 0)
    conv = _causal_conv(buf, xbc_ref[...], cw_ref[...], first) + cb_ref[...]
    xbc = jnp.where(valid, _silu(conv), 0.0)
    d_ssm = SSD_H * SSD_P
    gw = SSD_G * SSD_N
    x = xbc[:, :d_ssm]
    bm = xbc[:, d_ssm:d_ssm + gw]
    cm = xbc[:, d_ssm + gw:]

    row = _iota((CHUNK, CHUNK), 0)
    col = _iota((CHUNK, CHUNK), 1)
    causal = col <= row
    lane = _iota((1, CHUNK), 1)
    dlane = jnp.logical_and(lane >= SM_SSD_DT, lane < SM_SSD_DT + SSD_H)
    dt_tile = jnp.where(jnp.logical_and(dlane, valid), _softplus(sm_ref[...] + dtb_ref[...]), 0.0)
    da_tile = dt_tile * (-jnp.exp(alog_ref[...]))
    ac_all = _cumsum_rows(causal.astype(BF16), da_tile)
    ac_all_t = ac_all.T

    hp = SSD_H // SSD_G
    h_old = h_ref[...]
    ys, hs = [], []
    for g in range(SSD_G):
        bg = bm[:, g * SSD_N:(g + 1) * SSD_N]
        cg = cm[:, g * SSD_N:(g + 1) * SSD_N]
        cb = _dot_nt(cg, bg)
        for r in range(hp):
            h = g * hp + r
            al = SM_SSD_DT + h
            dt = dt_tile[:, al:al + 1]
            ac = ac_all[:, al:al + 1]
            ac_row = ac_all_t[al:al + 1, :]
            a_last = ac_all[CHUNK - 1:CHUNK, al:al + 1]
            xh = x[:, h * SSD_P:(h + 1) * SSD_P]
            xdt = xh * dt
            lmat = jnp.exp(jnp.where(causal, ac - ac_row, NEG))
            hprev = h_old[:, h * SSD_P:(h + 1) * SSD_P]
            y = _dot(cb * lmat, xdt) + _dot(cg, hprev) * jnp.exp(ac) + d_ref[h] * xh
            ys.append(y)
            xdec = xdt * jnp.exp(a_last - ac)
            hs.append(hprev * jnp.exp(a_last) + _dot_tn(bg, xdec))
    h_ref[...] = jnp.concatenate(hs, axis=-1)
    y = jnp.concatenate(ys, axis=-1) * _silu(z_ref[...])
    nw = nw_ref[...]
    gwid = d_ssm // SSD_G
    outs = [_rms(y[:, g * gwid:(g + 1) * gwid], nw[:, g * gwid:(g + 1) * gwid]) for g in range(SSD_G)]
    o_ref[...] = jnp.where(valid, jnp.concatenate(outs, axis=-1), 0.0).astype(o_ref.dtype)


def _ssd(xbc, z, sm, cw, cb, alog_row, dtb_row, d_skip, nw, bsz, nx):
    rows, cwid = xbc.shape
    d_ssm = SSD_H * SSD_P
    rb = lambda b, c: (_chunk_block(b, c, nx), 0)
    fixed = lambda b, c: (0, 0)
    return pl.pallas_call(
        functools.partial(_ssd_kernel, nx=nx),
        grid=(bsz, nx + 2),
        in_specs=[pl.BlockSpec(memory_space=pltpu.SMEM),
                  pl.BlockSpec((CHUNK, cwid), rb),
                  pl.BlockSpec((CHUNK, d_ssm), rb),
                  pl.BlockSpec((CHUNK, CHUNK), rb),
                  pl.BlockSpec((CONV_K, cwid), fixed),
                  pl.BlockSpec((1, cwid), fixed),
                  pl.BlockSpec((1, CHUNK), fixed),
                  pl.BlockSpec((1, CHUNK), fixed),
                  pl.BlockSpec((1, d_ssm), fixed)],
        out_specs=pl.BlockSpec((CHUNK, d_ssm), rb),
        out_shape=jax.ShapeDtypeStruct((rows, d_ssm), BF16),
        scratch_shapes=[pltpu.VMEM((CHUNK + 8, cwid), F32),
                        pltpu.VMEM((SSD_N, d_ssm), F32)],
        compiler_params=pltpu.CompilerParams(dimension_semantics=("arbitrary", "arbitrary")),
        name="ssd",
    )(d_skip, xbc, z, sm, cw, cb, alog_row, dtb_row, nw)


def _outproj_kernel(h_ref, y0_ref, y1_ref, y2_ref, y3_ref, w_ref, o_ref):
    acc = h_ref[...]
    for i, y_ref in enumerate((y0_ref, y1_ref, y2_ref, y3_ref)):
        wd = y_ref.shape[1]
        acc = acc + jnp.dot(y_ref[...].astype(BF16), w_ref[i * wd:(i + 1) * wd, :],
                            preferred_element_type=F32)
    o_ref[...] = acc


def _outproj(h, ys, w):
    rows, d = h.shape
    tm = TM_OUT
    wd = ys[0].shape[1]
    return pl.pallas_call(
        _outproj_kernel,
        grid=(rows // tm,),
        in_specs=[pl.BlockSpec((tm, d), lambda i: (i, 0))]
                 + [pl.BlockSpec((tm, wd), lambda i: (i, 0)) for _ in ys]
                 + [pl.BlockSpec(w.shape, lambda i: (0, 0))],
        out_specs=pl.BlockSpec((tm, d), lambda i: (i, 0)),
        out_shape=jax.ShapeDtypeStruct((rows, d), F32),
        compiler_params=pltpu.CompilerParams(dimension_semantics=("arbitrary",),
                                             vmem_limit_bytes=VMEM_LIMIT),
        name="outproj",
    )(h, *ys, w)


def _ffn_kernel(h_ref, nw_ref, wg_ref, wu_ref, wd_ref, o_ref, u_scr, acc_scr):
    f = pl.program_id(1)

    @pl.when(f == 0)
    def _():
        u_scr[...] = _rms(h_ref[...], nw_ref[...]).astype(BF16)
        acc_scr[...] = jnp.zeros_like(acc_scr)

    u = u_scr[...]
    g = jnp.dot(u, wg_ref[...], preferred_element_type=F32)
    up = jnp.dot(u, wu_ref[...], preferred_element_type=F32)
    a = (_silu(g) * up).astype(BF16)
    acc_scr[...] += jnp.dot(a, wd_ref[...], preferred_element_type=F32)

    @pl.when(f == pl.num_programs(1) - 1)
    def _():
        o_ref[...] = h_ref[...] + acc_scr[...]


def _ffn(h, nw, wg, wu, wd):
    rows, d = h.shape
    dff = wg.shape[1]
    tm, tf = TM_FFN, TF_DENSE
    return pl.pallas_call(
        _ffn_kernel,
        grid=(rows // tm, dff // tf),
        in_specs=[pl.BlockSpec((tm, d), lambda i, f: (i, 0)),
                  pl.BlockSpec((1, d), lambda i, f: (0, 0)),
                  pl.BlockSpec((d, tf), lambda i, f: (0, f)),
                  pl.BlockSpec((d, tf), lambda i, f: (0, f)),
                  pl.BlockSpec((tf, d), lambda i, f: (f, 0))],
        out_specs=pl.BlockSpec((tm, d), lambda i, f: (i, 0)),
        out_shape=jax.ShapeDtypeStruct((rows, d), F32),
        scratch_shapes=[pltpu.VMEM((tm, d), BF16), pltpu.VMEM((tm, d), F32)],
        compiler_params=pltpu.CompilerParams(dimension_semantics=("arbitrary", "arbitrary"),
                                             vmem_limit_bytes=VMEM_LIMIT),
        name="ffn_dense",
    )(h, nw, wg, wu, wd)


def _router_kernel(h_ref, nw_ref, wr_ref, u_ref, r_ref):
    u = _rms(h_ref[...], nw_ref[...])
    u_ref[...] = u
    lane = _iota((1, CHUNK), 1)
    logits = jnp.where(lane < N_EXP, _dot_hi(u, wr_ref[...]), -jnp.inf)
    m1 = jnp.max(logits, axis=-1, keepdims=True)
    i1 = jnp.min(jnp.where(logits == m1, lane, CHUNK), axis=-1, keepdims=True)
    rest = jnp.where(lane == i1, -jnp.inf, logits)
    m2 = jnp.max(rest, axis=-1, keepdims=True)
    i2 = jnp.min(jnp.where(rest == m2, lane, CHUNK), axis=-1, keepdims=True)
    e = jnp.exp(m2 - m1)
    g1 = 1.0 / (1.0 + e)
    g2 = e / (1.0 + e)
    out = jnp.where(lane == 0, i1.astype(F32), 0.0)
    out = jnp.where(lane == 1, i2.astype(F32), out)
    out = jnp.where(lane == 2, g1, out)
    out = jnp.where(lane == 3, g2, out)
    r_ref[...] = out


def _router(h, nw, wr):
    rows, d = h.shape
    tm = TM_OUT
    return pl.pallas_call(
        _router_kernel,
        grid=(rows // tm,),
        in_specs=[pl.BlockSpec((tm, d), lambda i: (i, 0)),
                  pl.BlockSpec((1, d), lambda i: (0, 0)),
                  pl.BlockSpec((d, CHUNK), lambda i: (0, 0))],
        out_specs=[pl.BlockSpec((tm, d), lambda i: (i, 0)),
                   pl.BlockSpec((tm, CHUNK), lambda i: (i, 0))],
        out_shape=[jax.ShapeDtypeStruct((rows, d), F32),
                   jax.ShapeDtypeStruct((rows, CHUNK), F32)],
        compiler_params=pltpu.CompilerParams(dimension_semantics=("arbitrary",)),
        name="moe_router",
    )(h, nw, wr)


def _expert_kernel(be_ref, nu_ref, tok_ref, u_hbm, wg_ref, wu_ref, wd_ref, o_ref, acc_scr, xraw, xbf, sems):
    i = pl.program_id(0)
    f = pl.program_id(1)
    last = f == pl.num_programs(1) - 1
    n_used = nu_ref[0]
    used = i < n_used
    tm = xbf.shape[0]

    def gather_block(j, slot):
        def issue(r, carry):
            t = tok_ref[j * tm + r]
            pltpu.make_async_copy(u_hbm.at[pl.ds(t, 1), :], xraw.at[slot, pl.ds(r, 1), :],
                                  sems.at[slot]).start()
            return carry
        lax.fori_loop(0, tm, issue, 0, unroll=8)

    @pl.when(jnp.logical_and(used, f == 0))
    def _():
        slot = i % 2

        @pl.when(i == 0)
        def _():
            gather_block(0, 0)

        pltpu.make_async_copy(u_hbm.at[pl.ds(0, tm), :], xraw.at[slot], sems.at[slot]).wait()
        xbf[...] = xraw[slot].astype(BF16)
        acc_scr[...] = jnp.zeros_like(acc_scr)

        @pl.when(i + 1 < n_used)
        def _():
            gather_block(i + 1, 1 - slot)

    @pl.when(used)
    def _():
        x = xbf[...]
        g = jnp.dot(x, wg_ref[0], preferred_element_type=F32)
        up = jnp.dot(x, wu_ref[0], preferred_element_type=F32)
        a = (_silu(g) * up).astype(BF16)
        acc_scr[...] += jnp.dot(a, wd_ref[0], preferred_element_type=F32)

    @pl.when(jnp.logical_and(used, last))
    def _():
        o_ref[...] = acc_scr[...]

    @pl.when(jnp.logical_and(jnp.logical_not(used), last))
    def _():
        o_ref[...] = jnp.zeros_like(o_ref)


def _experts(blk_exp, n_used, buf_tok, u, wg, wu, wd):
    cap = buf_tok.shape[0]
    d = u.shape[1]
    dff = wg.shape[2]
    tm, tf = TM_MOE, TF_MOE
    return pl.pallas_call(
        _expert_kernel,
        grid_spec=pltpu.PrefetchScalarGridSpec(
            num_scalar_prefetch=3,
            grid=(cap // tm, dff // tf),
            in_specs=[pl.BlockSpec(memory_space=pl.ANY),
                      pl.BlockSpec((1, d, tf), lambda i, f, be, nu, tok: (be[i], 0, f)),
                      pl.BlockSpec((1, d, tf), lambda i, f, be, nu, tok: (be[i], 0, f)),
                      pl.BlockSpec((1, tf, d), lambda i, f, be, nu, tok: (be[i], f, 0))],
            out_specs=pl.BlockSpec((tm, d), lambda i, f, be, nu, tok: (i, 0)),
            scratch_shapes=[pltpu.VMEM((tm, d), F32),
                            pltpu.VMEM((2, tm, d), F32),
                            pltpu.VMEM((tm, d), BF16),
                            pltpu.SemaphoreType.DMA((2,))]),
        out_shape=jax.ShapeDtypeStruct((cap, d), F32),
        compiler_params=pltpu.CompilerParams(dimension_semantics=("arbitrary", "arbitrary"),
                                             vmem_limit_bytes=VMEM_LIMIT),
        name="moe_experts",
    )(blk_exp, n_used, buf_tok, u, wg, wu, wd)


def _combine_kernel(d1_ref, d2_ref, h_ref, r_ref, nw_ref, yb_hbm, o_ref, buf1, buf2, sem, *, nx):
    b = pl.program_id(0)
    c = pl.program_id(1)
    base = (b * (nx + 2) + c) * CHUNK

    def issue(r, carry):
        pltpu.make_async_copy(yb_hbm.at[pl.ds(d1_ref[base + r], 1), :], buf1.at[pl.ds(r, 1), :], sem).start()
        pltpu.make_async_copy(yb_hbm.at[pl.ds(d2_ref[base + r], 1), :], buf2.at[pl.ds(r, 1), :], sem).start()
        return carry

    lax.fori_loop(0, CHUNK, issue, 0, unroll=8)
    pltpu.make_async_copy(yb_hbm.at[pl.ds(0, CHUNK), :], buf1, sem).wait()
    pltpu.make_async_copy(yb_hbm.at[pl.ds(0, CHUNK), :], buf2, sem).wait()
    r = r_ref[...]
    y = h_ref[...] + r[:, 2:3] * buf1[...] + r[:, 3:4] * buf2[...]
    o_ref[0] = _rms(y, nw_ref[...])


def _combine_final(d1, d2, h, route, nw, yb, bsz, nx):
    d = h.shape[1]
    seq = nx * CHUNK
    rb = lambda b, c, d1, d2: (b * (nx + 2) + c, 0)
    return pl.pallas_call(
        functools.partial(_combine_kernel, nx=nx),
        grid_spec=pltpu.PrefetchScalarGridSpec(
            num_scalar_prefetch=2,
            grid=(bsz, nx),
            in_specs=[pl.BlockSpec((CHUNK, d), rb),
                      pl.BlockSpec((CHUNK, CHUNK), rb),
                      pl.BlockSpec((1, d), lambda b, c, d1, d2: (0, 0)),
                      pl.BlockSpec(memory_space=pl.ANY)],
            out_specs=pl.BlockSpec((1, CHUNK, d), lambda b, c, d1, d2: (b, c, 0)),
            scratch_shapes=[pltpu.VMEM((CHUNK, d), F32), pltpu.VMEM((CHUNK, d), F32),
                            pltpu.SemaphoreType.DMA(())]),
        out_shape=jax.ShapeDtypeStruct((bsz, seq, d), F32),
        compiler_params=pltpu.CompilerParams(dimension_semantics=("arbitrary", "arbitrary")),
        name="moe_combine_final",
    )(d1, d2, h, route, nw, yb)


def _moe_slots(route, tm, nx):
    rows = route.shape[0]
    n_assign = 2 * rows
    pos = jnp.arange(rows, dtype=jnp.int32) % ((nx + 2) * CHUNK)
    real = jnp.logical_or(pos < nx * CHUNK,
                          jnp.logical_and(pos >= nx * CHUNK + PAD, pos < (nx + 1) * CHUNK))
    e = jnp.where(real[:, None], route[:, :2].astype(jnp.int32), N_EXP)
    flat_e = e.reshape(-1)
    onehot = (flat_e[:, None] == jnp.arange(N_EXP, dtype=jnp.int32)[None, :]).astype(jnp.int32)
    csum = jnp.cumsum(onehot, axis=0)
    rank = jnp.sum(csum * onehot, axis=1) - 1
    counts = csum[-1]
    padded = ((counts + tm - 1) // tm) * tm
    ends_p = jnp.cumsum(padded)
    start_p = ends_p - padded
    n_real = n_assign // ((nx + 2) * CHUNK) * (nx * CHUNK + N_META)
    nblk = -(-n_real // tm) + N_EXP
    cap = nblk * tm
    dest = jnp.where(flat_e < N_EXP, jnp.sum(start_p[None, :] * onehot, axis=1) + rank, cap)
    buf_tok = jnp.zeros((cap,), jnp.int32).at[dest].set(jnp.arange(n_assign, dtype=jnp.int32) // 2,
                                                        mode="drop")
    blk_start = jnp.arange(nblk, dtype=jnp.int32) * tm
    blk_exp = jnp.minimum(jnp.sum((ends_p[None, :] <= blk_start[:, None]).astype(jnp.int32), axis=-1),
                          N_EXP - 1).astype(jnp.int32)
    n_used = (ends_p[-1:] // tm).astype(jnp.int32)
    dest2 = dest.reshape(rows, 2)
    return buf_tok, blk_exp, n_used, dest2[:, 0], dest2[:, 1], cap


def _lane_row(vals, offset):
    return jnp.zeros((1, CHUNK), F32).at[0, offset:offset + vals.shape[0]].set(vals.astype(F32))


def kernel(x, meta_tokens, norm_mix, w_in, dn_conv_w, dn_a_log, dn_dt_bias, dn_norm_w, df_lambda, df_norm_w, sw_sinks, ssd_conv_w, ssd_conv_b, ssd_a_log, ssd_dt_bias, ssd_d, ssd_norm_w, w_out, norm_ffn, ffn_w_gate, ffn_w_up, ffn_w_down, moe_router, moe_w_gate, moe_w_up, moe_w_down, norm_final):
    bsz, seq, d = x.shape
    depth = w_in.shape[0]
    assert seq % TQ == 0 and depth == 2
    nx = seq // CHUNK
    rows = bsz * (nx + 2) * CHUNK

    meta = jnp.broadcast_to(meta_tokens[None].astype(F32), (bsz, N_META, d))
    h = jnp.concatenate([x.astype(F32), jnp.zeros((bsz, PAD, d), F32), meta,
                         jnp.zeros((bsz, CHUNK, d), F32)], axis=1).reshape(rows, d)

    y = None
    for l in range(depth):
        wl = w_in[l]
        w_cat = jnp.concatenate(
            [wl[:, 0:1024], wl[:, 1024:1032], wl[:, 3336:3340], jnp.zeros((d, CHUNK - 12), wl.dtype),
             wl[:, 1032:2312], wl[:, 2312:3336]], axis=1).astype(BF16)
        (dn_qkv, dn_z, small, df_q, df_k, df_v, sw_q, sw_kv, ssd_z, ssd_xbc) = _inproj(
            h, norm_mix[l][None].astype(F32), w_cat)

        y_dn = _deltanet(dn_qkv, dn_z, small, dn_conv_w[l].astype(F32),
                         _lane_row(dn_a_log[l], SM_DN_A), _lane_row(dn_dt_bias[l], SM_DN_A),
                         dn_norm_w[l][None].astype(F32), bsz, nx)
        lambda_init = 0.8 - 0.6 * math.exp(-0.3 * l)
        y_df = _diff_attention(df_q, df_k, df_v, df_lambda[l].astype(F32),
                               df_norm_w[l][:, None].astype(F32), lambda_init, bsz, nx)
        y_sw = _swa(sw_q, sw_kv, sw_sinks[l].astype(F32), bsz, nx)
        y_ssd = _ssd(ssd_xbc, ssd_z, small, ssd_conv_w[l].astype(F32), ssd_conv_b[l][None].astype(F32),
                     _lane_row(ssd_a_log[l], SM_SSD_DT), _lane_row(ssd_dt_bias[l], SM_SSD_DT),
                     ssd_d[l].astype(F32), ssd_norm_w[l][None].astype(F32), bsz, nx)
        h = _outproj(h, (y_dn, y_df, y_sw, y_ssd), w_out[l].astype(BF16))

        i = l // 2
        if l % 2 == 0:
            h = _ffn(h, norm_ffn[l][None].astype(F32), ffn_w_gate[i].astype(BF16),
                     ffn_w_up[i].astype(BF16), ffn_w_down[i].astype(BF16))
        else:
            wr = jnp.zeros((d, CHUNK), F32).at[:, :N_EXP].set(moe_router[i].astype(F32))
            u, route = _router(h, norm_ffn[l][None].astype(F32), wr)
            buf_tok, blk_exp, n_used, d1, d2, cap = _moe_slots(route, TM_MOE, nx)
            yb = _experts(blk_exp, n_used, buf_tok, u, moe_w_gate[i].astype(BF16),
                          moe_w_up[i].astype(BF16), moe_w_down[i].astype(BF16))
            y = _combine_final(d1, d2, h, route, norm_final[None].astype(F32), yb, bsz, nx)
    return y
```

```python
import functools
import math

import jax
import jax.numpy as jnp
from jax import lax
from jax.experimental import pallas as pl
from jax.experimental.pallas import tpu as pltpu

F32 = jnp.float32
BF16 = jnp.bfloat16

N_META = 16
CHUNK = 128
PAD = CHUNK - N_META
EPS = 1e-6
NEG = -1e30
CONV_K = 4

DN_H, DN_D = 4, 64
DF_H, DF_DQK, DF_DV = 4, 32, 64
SW_HQ, SW_HKV, SW_D = 4, 2, 64
SSD_H, SSD_P, SSD_N, SSD_G = 4, 64, 128, 2
N_EXP = 8

SM_DN_B, SM_DN_A, SM_SSD_DT = 0, 4, 8

VMEM_LIMIT = 56 * 1024 * 1024

TM_PROJ = 256
TM_OUT = 512
TM_FFN = 512
TF_DENSE = 1408
TM_MOE = 512
TF_MOE = 896
TQ = 2 * CHUNK

HI = lax.Precision.HIGHEST


def _dot(a, b):
    return jnp.dot(a.astype(BF16), b.astype(BF16), preferred_element_type=F32)


def _dot_nt(a, b):
    return lax.dot_general(a.astype(BF16), b.astype(BF16), (((1,), (1,)), ((), ())),
                           preferred_element_type=F32)


def _dot_tn(a, b):
    return lax.dot_general(a.astype(BF16), b.astype(BF16), (((0,), (0,)), ((), ())),
                           preferred_element_type=F32)


def _dot_hi(a, b):
    return jnp.dot(a, b, precision=HI, preferred_element_type=F32)


def _split(x):
    hi = x.astype(BF16)
    return hi, (x - hi.astype(F32)).astype(BF16)


def _dot_x3(a, b):
    a_hi, a_lo = a
    b_hi, b_lo = b
    n = b_hi.shape[1]
    lhs = jnp.concatenate([a_hi, a_lo], axis=1)
    rhs = jnp.concatenate([jnp.concatenate([b_hi, b_lo], axis=1),
                           jnp.concatenate([b_hi, jnp.zeros_like(b_lo)], axis=1)], axis=0)
    o = jnp.dot(lhs, rhs, preferred_element_type=F32)
    return o[:, :n] + o[:, n:]


def _cumsum_rows(ltri_bf16, x):
    hi, lo = _split(x)
    n = x.shape[1]
    o = jnp.dot(ltri_bf16, jnp.concatenate([hi, lo], axis=1), preferred_element_type=F32)
    return o[:, :n] + o[:, n:]


def _sigmoid(x):
    return 1.0 / (1.0 + jnp.exp(-x))


def _silu(x):
    return x * _sigmoid(x)


def _softplus(x):
    return jnp.maximum(x, 0.0) + jnp.log1p(jnp.exp(-jnp.abs(x)))


def _rms(x, w):
    return x * lax.rsqrt(jnp.mean(x * x, axis=-1, keepdims=True) + EPS) * w


def _iota(shape, dim):
    return lax.broadcasted_iota(jnp.int32, shape, dim)


def _causal_conv(buf, x, cw, first):
    width = x.shape[1]

    @pl.when(first)
    def _():
        buf[0:8, :] = jnp.zeros((8, width), F32)

    buf[8:8 + CHUNK, :] = x
    conv = cw[CONV_K - 1:CONV_K, :] * x
    for j in range(CONV_K - 1):
        lo = 8 - (CONV_K - 1) + j
        conv = conv + cw[j:j + 1, :] * buf[lo:lo + CHUNK, :]
    buf[0:8, :] = buf[CHUNK:CHUNK + 8, :]
    return conv


def _chunk_block(b, c, nx):
    return b * (nx + 2) + jnp.where(c == 0, nx, jnp.where(c > nx, c, c - 1))


def _inproj_kernel(h_ref, nw_ref, w_ref, *out_refs, widths):
    u = _rms(h_ref[...], nw_ref[...])
    p = jnp.dot(u.astype(BF16), w_ref[...], preferred_element_type=F32)
    off = 0
    for o_ref, w in zip(out_refs, widths):
        o_ref[...] = p[:, off:off + w].astype(o_ref.dtype)
        off += w


PROJ_WIDTHS = (768, 256, 128, 256, 256, 256, 256, 256, 256, 768)
PROJ_DTYPES = (F32, F32, F32, BF16, BF16, BF16, BF16, BF16, F32, F32)


def _inproj(h, nw, w):
    rows, d = h.shape
    ncol = w.shape[1]
    tm = TM_PROJ
    return pl.pallas_call(
        functools.partial(_inproj_kernel, widths=PROJ_WIDTHS),
        grid=(rows // tm,),
        in_specs=[pl.BlockSpec((tm, d), lambda i: (i, 0)),
                  pl.BlockSpec((1, d), lambda i: (0, 0)),
                  pl.BlockSpec((d, ncol), lambda i: (0, 0))],
        out_specs=[pl.BlockSpec((tm, wd), lambda i: (i, 0)) for wd in PROJ_WIDTHS],
        out_shape=[jax.ShapeDtypeStruct((rows, wd), dt) for wd, dt in zip(PROJ_WIDTHS, PROJ_DTYPES)],
        compiler_params=pltpu.CompilerParams(dimension_semantics=("arbitrary",),
                                             vmem_limit_bytes=VMEM_LIMIT),
        name="inproj",
    )(h, nw, w)


def _dn_kernel(qkv_ref, z_ref, sm_ref, cw_ref, alog_ref, dtb_ref, nw_ref, o_ref, *scratch, nx):
    c = pl.program_id(1)

    @pl.when(c > nx)
    def _():
        o_ref[...] = jnp.zeros_like(o_ref)

    @pl.when(c <= nx)
    def _():
        _dn_chunk(c, qkv_ref, z_ref, sm_ref, cw_ref, alog_ref, dtb_ref, nw_ref, o_ref, *scratch)


def _put_lhs(ref, i, x):
    hi, lo = _split(x)
    ref[i] = jnp.concatenate([hi, lo], axis=1)


def _put_rhs(ref, i, x):
    hi, lo = _split(x)
    ref[i] = jnp.concatenate([jnp.concatenate([hi, lo], axis=1),
                              jnp.concatenate([hi, jnp.zeros_like(lo)], axis=1)], axis=0)


def _mm_x3(l_ref, li, r_ref, ri):
    o = jnp.dot(l_ref[li], r_ref[ri], preferred_element_type=F32)
    return o[:, :CHUNK] + o[:, CHUNK:]


def _dn_chunk(c, qkv_ref, z_ref, sm_ref, cw_ref, alog_ref, dtb_ref, nw_ref, o_ref, buf, s_ref,
              a_scr, t_scr, lp_scr, rp_scr, lt_scr, rt_scr, rr_scr, attn_scr, qg_scr, k_scr):
    first = c == 0

    @pl.when(first)
    def _():
        s_ref[...] = jnp.zeros_like(s_ref)

    qkv = _silu(_causal_conv(buf, qkv_ref[...], cw_ref[...], first))
    hd = DN_H * DN_D

    row1 = _iota((CHUNK, 1), 0)
    valid = jnp.logical_or(row1 >= PAD, c > 0)
    row = _iota((CHUNK, CHUNK), 0)
    col = _iota((CHUNK, CHUNK), 1)
    causal = col <= row
    strict = col < row
    eye = (col == row).astype(F32)
    rb16, cb16 = row >> 4, col >> 4
    blk16 = rb16 == cb16
    joins = [jnp.logical_and((rb16 >> (j + 1)) == (cb16 >> (j + 1)), (rb16 >> j) != (cb16 >> j))
             for j in range(3)]

    sm = sm_ref[...]
    lane = _iota((1, CHUNK), 1)
    glane = jnp.logical_and(lane >= SM_DN_A, lane < SM_DN_A + DN_H)
    g_tile = -jnp.exp(alog_ref[...]) * _softplus(sm + dtb_ref[...])
    g_tile = jnp.where(jnp.logical_and(glane, valid), g_tile, 0.0)
    gc_all = _cumsum_rows(causal.astype(BF16), g_tile)
    gc_all_t = gc_all.T

    z = z_ref[...]
    nw = nw_ref[...]
    outs = []
    heads = range(DN_H)
    for h in heads:
        lo = h * DN_D
        q = qkv[:, lo:lo + DN_D]
        k = qkv[:, hd + lo:hd + lo + DN_D]
        v = qkv[:, 2 * hd + lo:2 * hd + lo + DN_D]
        q = q * lax.rsqrt(jnp.sum(q * q, axis=-1, keepdims=True) + EPS) * (DN_D ** -0.5)
        k = k * lax.rsqrt(jnp.sum(k * k, axis=-1, keepdims=True) + EPS)
        beta = _sigmoid(sm[:, SM_DN_B + h:SM_DN_B + h + 1])
        gl = SM_DN_A + h
        gc = gc_all[:, gl:gl + 1]
        gc_row = gc_all_t[gl:gl + 1, :]
        decay = jnp.exp(jnp.where(causal, gc - gc_row, NEG))
        kb = k * beta
        a = jnp.where(strict, _dot_nt(kb, k) * decay, 0.0)
        egc = jnp.exp(gc)
        a_scr[h] = a
        _put_rhs(rr_scr, h, jnp.concatenate([v * beta, kb * egc], axis=-1))
        attn_scr[h] = (_dot_nt(q, k) * decay).astype(BF16)
        qg_scr[h] = q * egc
        k_scr[h] = k
        p1 = jnp.where(blk16, -a, 0.0)
        _put_lhs(lp_scr, h, p1)
        _put_rhs(rp_scr, h, p1)
        t_scr[h] = eye + p1
        _put_lhs(lt_scr, h, eye + p1)
    nh = DN_H
    for h in heads:
        p2 = _mm_x3(lp_scr, h, rp_scr, h)
        _put_lhs(lp_scr, nh + h, p2)
        _put_rhs(rp_scr, nh + h, p2)
    for h in heads:
        t = t_scr[h] + _mm_x3(lt_scr, h, rp_scr, nh + h)
        p4 = _mm_x3(lp_scr, nh + h, rp_scr, nh + h)
        t_scr[nh + h] = t
        _put_lhs(lt_scr, nh + h, t)
        _put_lhs(lp_scr, 2 * nh + h, p4)
        _put_rhs(rp_scr, 2 * nh + h, p4)
    for h in heads:
        t = t_scr[nh + h] + _mm_x3(lt_scr, nh + h, rp_scr, 2 * nh + h)
        p8 = _mm_x3(lp_scr, 2 * nh + h, rp_scr, 2 * nh + h)
        t_scr[2 * nh + h] = t
        _put_lhs(lt_scr, 2 * nh + h, t)
        _put_rhs(rp_scr, 3 * nh + h, p8)
    for h in heads:
        t = t_scr[2 * nh + h] + _mm_x3(lt_scr, 2 * nh + h, rp_scr, 3 * nh + h)
        t_scr[3 * nh + h] = t
        _put_lhs(lt_scr, 3 * nh + h, t)
        _put_rhs(rt_scr, h, t)
    for j, jm in enumerate(joins):
        src, dst = (3 + j) * nh, (4 + j) * nh
        for h in heads:
            _put_rhs(rp_scr, dst + h, jnp.where(jm, a_scr[h], 0.0))
            _put_lhs(lp_scr, src + h, _mm_x3(lt_scr, src + h, rp_scr, dst + h))
        for h in heads:
            t = t_scr[src + h] - _mm_x3(lp_scr, src + h, rt_scr, j * nh + h)
            t_scr[dst + h] = t
            _put_lhs(lt_scr, dst + h, t)
            if j + 1 < len(joins):
                _put_rhs(rt_scr, (j + 1) * nh + h, t)
    last = (3 + len(joins)) * nh
    for h in heads:
        lo = h * DN_D
        gl = SM_DN_A + h
        gc = gc_all[:, gl:gl + 1]
        g_last = gc_all[CHUNK - 1:CHUNK, gl:gl + 1]
        sol = _mm_x3(lt_scr, last + h, rr_scr, h)
        u = sol[:, :DN_D]
        w = sol[:, DN_D:]
        s_old = s_ref[h]
        v_new = u - _dot(w, s_old)
        o = _dot(qg_scr[h], s_old) + jnp.dot(attn_scr[h], v_new.astype(BF16), preferred_element_type=F32)
        kdec = k_scr[h] * jnp.exp(g_last - gc)
        s_ref[h] = s_old * jnp.exp(g_last) + _dot_tn(kdec, v_new)
        o = _rms(o, nw) * _silu(z[:, lo:lo + DN_D])
        outs.append(jnp.where(valid, o, 0.0))
    o_ref[...] = jnp.concatenate(outs, axis=-1).astype(o_ref.dtype)


def _deltanet(qkv, z, sm, cw, alog_row, dtb_row, nw, bsz, nx):
    rows = qkv.shape[0]
    rb = lambda b, c: (_chunk_block(b, c, nx), 0)
    fixed = lambda b, c: (0, 0)
    return pl.pallas_call(
        functools.partial(_dn_kernel, nx=nx),
        grid=(bsz, nx + 2),
        in_specs=[pl.BlockSpec((CHUNK, 3 * DN_H * DN_D), rb),
                  pl.BlockSpec((CHUNK, DN_H * DN_D), rb),
                  pl.BlockSpec((CHUNK, CHUNK), rb),
                  pl.BlockSpec((CONV_K, 3 * DN_H * DN_D), fixed),
                  pl.BlockSpec((1, CHUNK), fixed),
                  pl.BlockSpec((1, CHUNK), fixed),
                  pl.BlockSpec((1, DN_D), fixed)],
        out_specs=pl.BlockSpec((CHUNK, DN_H * DN_D), rb),
        out_shape=jax.ShapeDtypeStruct((rows, DN_H * DN_D), BF16),
        scratch_shapes=[pltpu.VMEM((CHUNK + 8, 3 * DN_H * DN_D), F32),
                        pltpu.VMEM((DN_H, DN_D, DN_D), F32),
                        pltpu.VMEM((DN_H, CHUNK, CHUNK), F32),
                        pltpu.VMEM((7 * DN_H, CHUNK, CHUNK), F32),
                        pltpu.VMEM((6 * DN_H, CHUNK, 2 * CHUNK), BF16),
                        pltpu.VMEM((7 * DN_H, 2 * CHUNK, 2 * CHUNK), BF16),
                        pltpu.VMEM((7 * DN_H, CHUNK, 2 * CHUNK), BF16),
                        pltpu.VMEM((3 * DN_H, 2 * CHUNK, 2 * CHUNK), BF16),
                        pltpu.VMEM((DN_H, 2 * CHUNK, 2 * CHUNK), BF16),
                        pltpu.VMEM((DN_H, CHUNK, CHUNK), BF16),
                        pltpu.VMEM((DN_H, CHUNK, DN_D), F32),
                        pltpu.VMEM((DN_H, CHUNK, DN_D), F32)],
        compiler_params=pltpu.CompilerParams(dimension_semantics=("arbitrary", "arbitrary"),
                                             vmem_limit_bytes=VMEM_LIMIT),
        name="deltanet",
    )(qkv, z, sm, cw, alog_row, dtb_row, nw)


def _df_kernel(q_ref, k_ref, v_ref, lam_ref, nw_ref, o_ref, vt_scr, qm_scr, s_scr, p_scr, m_scr, l_scr,
               acc_scr, *, lambda_init, nx):
    i = pl.program_id(1)
    nt = nx * CHUNK // TQ
    xrows = nx * CHUNK
    nhm = 2 * DF_H
    width = nhm * DF_DQK

    @pl.when(i == 0)
    def _():
        for t in range(nt + 1):
            vt_scr[t] = v_ref[t * TQ:(t + 1) * TQ, :].astype(F32).T.astype(BF16)

    qt = q_ref[...].astype(F32).T * (DF_DQK ** -0.5 * math.log2(math.e))
    frow = _iota((width, TQ), 0)
    for hm in range(nhm):
        sel = jnp.logical_and(frow >= hm * DF_DQK, frow < (hm + 1) * DF_DQK)
        qm_scr[hm] = jnp.where(sel, qt, 0.0).astype(BF16)
    m_scr[...] = jnp.full(m_scr.shape, NEG, F32)
    l_scr[...] = jnp.zeros(l_scr.shape, F32)
    acc_scr[...] = jnp.zeros(acc_scr.shape, F32)

    def tile(kt, vt_of, mask):
        tk = kt.shape[0]
        for hm in range(nhm):
            s = jnp.dot(kt, qm_scr[hm], preferred_element_type=F32)
            if mask is not None:
                s = jnp.where(mask, s, NEG)
            s_scr[hm, 0:tk, :] = s
        for hm in range(nhm):
            m_old = m_scr[hm]
            m_new = jnp.maximum(m_old, jnp.max(s_scr[hm, 0:tk, :], axis=0, keepdims=True))
            m_scr[hm] = m_new
            alpha = jnp.exp2(m_old - m_new)
            p = jnp.exp2(s_scr[hm, 0:tk, :] - m_new)
            l_scr[hm] = alpha * l_scr[hm] + jnp.sum(p, axis=0, keepdims=True)
            p_scr[hm, 0:tk, :] = p.astype(BF16)
            acc_scr[hm] = alpha * acc_scr[hm]
        for hm in range(nhm):
            acc_scr[hm] += jnp.dot(vt_of(hm // 2), p_scr[hm, 0:tk, :], preferred_element_type=F32)

    is_x = i < nt
    krow = _iota((CHUNK, TQ), 0)
    qcol = _iota((CHUNK, TQ), 1)
    head_mask = jnp.logical_and(krow >= PAD, jnp.logical_or(is_x, krow <= qcol))
    tile(k_ref[xrows:xrows + CHUNK, :],
         lambda h: vt_scr[nt, h * DF_DV:(h + 1) * DF_DV, 0:CHUNK], head_mask)

    def body(j, carry):
        off = pl.multiple_of(j * TQ, TQ)
        tile(k_ref[pl.ds(off, TQ), :], lambda h: vt_scr[j, h * DF_DV:(h + 1) * DF_DV, :], None)
        return carry

    lax.fori_loop(0, jnp.where(is_x, i, 0), body, 0)

    @pl.when(is_x)
    def _():
        off = pl.multiple_of(i * TQ, TQ)
        diag = _iota((TQ, TQ), 0) <= _iota((TQ, TQ), 1)
        tile(k_ref[pl.ds(off, TQ), :], lambda h: vt_scr[i, h * DF_DV:(h + 1) * DF_DV, :], diag)

    lp = lam_ref[...]
    lam = (jnp.exp(jnp.sum(lp[0:1] * lp[1:2], axis=-1, keepdims=True))
           - jnp.exp(jnp.sum(lp[2:3] * lp[3:4], axis=-1, keepdims=True)) + lambda_init)
    nw = nw_ref[...]
    outs = []
    for h in range(DF_H):
        o = acc_scr[2 * h] / l_scr[2 * h] - lam * (acc_scr[2 * h + 1] / l_scr[2 * h + 1])
        o = o * lax.rsqrt(jnp.mean(o * o, axis=0, keepdims=True) + EPS) * nw * (1.0 - lambda_init)
        outs.append(o)
    out = jnp.concatenate(outs, axis=0).T
    r = _iota((TQ, 1), 0)
    valid = jnp.logical_or(is_x, jnp.logical_and(r >= PAD, r < CHUNK))
    o_ref[...] = jnp.where(valid, out, 0.0).astype(o_ref.dtype)


def _diff_attention(q, k, v, lam_p, nw_col, lambda_init, bsz, nx):
    rows, width = q.shape
    lps = (nx + 2) * CHUNK
    nt = nx * CHUNK // TQ
    nhm = 2 * DF_H
    return pl.pallas_call(
        functools.partial(_df_kernel, lambda_init=lambda_init, nx=nx),
        grid=(bsz, nt + 1),
        in_specs=[pl.BlockSpec((TQ, width), lambda b, i: (b * (nt + 1) + i, 0)),
                  pl.BlockSpec((lps, width), lambda b, i: (b, 0)),
                  pl.BlockSpec((lps, width), lambda b, i: (b, 0)),
                  pl.BlockSpec((4, DF_DQK), lambda b, i: (0, 0)),
                  pl.BlockSpec((DF_DV, 1), lambda b, i: (0, 0))],
        out_specs=pl.BlockSpec((TQ, width), lambda b, i: (b * (nt + 1) + i, 0)),
        out_shape=jax.ShapeDtypeStruct((rows, width), BF16),
        scratch_shapes=[pltpu.VMEM((nt + 1, width, TQ), BF16),
                        pltpu.VMEM((nhm, width, TQ), BF16),
                        pltpu.VMEM((nhm, TQ, TQ), F32),
                        pltpu.VMEM((nhm, TQ, TQ), BF16),
                        pltpu.VMEM((nhm, 1, TQ), F32),
                        pltpu.VMEM((nhm, 1, TQ), F32),
                        pltpu.VMEM((nhm, DF_DV, TQ), F32)],
        compiler_params=pltpu.CompilerParams(dimension_semantics=("arbitrary", "arbitrary"),
                                             vmem_limit_bytes=VMEM_LIMIT),
        name="diff_attention",
    )(q, k, v, lam_p, nw_col)


def _sw_kernel(sink_ref, q_ref, kv_ref, kvp_ref, o_ref, *, nx):
    c = pl.program_id(1)

    @pl.when(c > nx)
    def _():
        o_ref[...] = jnp.zeros_like(o_ref)

    @pl.when(c <= nx)
    def _():
        _sw_chunk(c, sink_ref, q_ref, kv_ref, kvp_ref, o_ref)


def _sw_chunk(c, sink_ref, q_ref, kv_ref, kvp_ref, o_ref):
    q = q_ref[...].astype(F32) * (SW_D ** -0.5)
    kv = kv_ref[...]
    kvp = kvp_ref[...]
    row = _iota((CHUNK, 2 * CHUNK), 0)
    col = _iota((CHUNK, 2 * CHUNK), 1)
    rel = row - col + CHUNK
    kpos = c * CHUNK - CHUNK + col
    msk = jnp.logical_and(jnp.logical_and(rel >= 0, rel < CHUNK), kpos >= PAD)
    valid = jnp.logical_or(_iota((CHUNK, 1), 0) >= PAD, c > 0)
    kvw = SW_HKV * SW_D
    outs = []
    for g in range(SW_HKV):
        kk = jnp.concatenate([kvp[:, g * SW_D:(g + 1) * SW_D], kv[:, g * SW_D:(g + 1) * SW_D]], axis=0)
        vv = jnp.concatenate([kvp[:, kvw + g * SW_D:kvw + (g + 1) * SW_D],
                              kv[:, kvw + g * SW_D:kvw + (g + 1) * SW_D]], axis=0)
        for r in range(SW_HQ // SW_HKV):
            hq = g * (SW_HQ // SW_HKV) + r
            sink = sink_ref[hq]
            s = jnp.where(msk, _dot_nt(q[:, hq * SW_D:(hq + 1) * SW_D], kk), NEG)
            m = jnp.maximum(jnp.max(s, axis=-1, keepdims=True), sink)
            p = jnp.exp(s - m)
            denom = jnp.sum(p, axis=-1, keepdims=True) + jnp.exp(sink - m)
            o = _dot(p, vv) / denom
            outs.append(jnp.where(valid, o, 0.0))
    o_ref[...] = jnp.concatenate(outs, axis=-1).astype(o_ref.dtype)


def _swa(q, kv, sinks, bsz, nx):
    rows, width = q.shape
    rb = lambda b, c: (_chunk_block(b, c, nx), 0)
    rbp = lambda b, c: (_chunk_block(b, jnp.maximum(c - 1, 0), nx), 0)
    return pl.pallas_call(
        functools.partial(_sw_kernel, nx=nx),
        grid=(bsz, nx + 2),
        in_specs=[pl.BlockSpec(memory_space=pltpu.SMEM),
                  pl.BlockSpec((CHUNK, width), rb),
                  pl.BlockSpec((CHUNK, width), rb),
                  pl.BlockSpec((CHUNK, width), rbp)],
        out_specs=pl.BlockSpec((CHUNK, width), rb),
        out_shape=jax.ShapeDtypeStruct((rows, width), BF16),
        compiler_params=pltpu.CompilerParams(dimension_semantics=("arbitrary", "arbitrary")),
        name="swa",
    )(sinks, q, kv, kv)


def _ssd_kernel(d_ref, xbc_ref, z_ref, sm_ref, cw_ref, cb_ref, alog_ref, dtb_ref, nw_ref, o_ref,
                buf, h_ref, *, nx):
    c = pl.program_id(1)

    @pl.when(c > nx)
    def _():
        o_ref[...] = jnp.zeros_like(o_ref)

    @pl.when(c <= nx)
    def _():
        _ssd_chunk(c, d_ref, xbc_ref, z_ref, sm_ref, cw_ref, cb_ref, alog_ref, dtb_ref, nw_ref, o_ref,
                   buf, h_ref)


def _ssd_chunk(c, d_ref, xbc_ref, z_ref, sm_ref, cw_ref, cb_ref, alog_ref, dtb_ref, nw_ref, o_ref,
               buf, h_ref):
    first = c == 0

    @pl.when(first)
    def _():
        h_ref[...] = jnp.zeros_like(h_ref)

    valid = jnp.logical_or(_iota((CHUNK, 1), 0) >= PAD, c > 0)
    conv = _causal_conv(buf, xbc_ref[...], cw_ref[...], first) + cb_ref[...]
    xbc = jnp.where(valid, _silu(conv), 0.0)
    d_ssm = SSD_H * SSD_P
    gw = SSD_G * SSD_N
    x = xbc[:, :d_ssm]
    bm = xbc[:, d_ssm:d_ssm + gw]
    cm = xbc[:, d_ssm + gw:]

    row = _iota((CHUNK, CHUNK), 0)
    col = _iota((CHUNK, CHUNK), 1)
    causal = col <= row
    lane = _iota((1, CHUNK), 1)
    dlane = jnp.logical_and(lane >= SM_SSD_DT, lane < SM_SSD_DT + SSD_H)
    dt_tile = jnp.where(jnp.logical_and(dlane, valid), _softplus(sm_ref[...] + dtb_ref[...]), 0.0)
    da_tile = dt_tile * (-jnp.exp(alog_ref[...]))
    ac_all = _cumsum_rows(causal.astype(BF16), da_tile)
    ac_all_t = ac_all.T

    hp = SSD_H // SSD_G
    h_old = h_ref[...]
    ys, hs = [], []
    for g in range(SSD_G):
        bg = bm[:, g * SSD_N:(g + 1) * SSD_N]
        cg = cm[:, g * SSD_N:(g + 1) * SSD_N]
        cb = _dot_nt(cg, bg)
        for r in range(hp):
            h = g * hp + r
            al = SM_SSD_DT + h
            dt = dt_tile[:, al:al + 1]
            ac = ac_all[:, al:al + 1]
            ac_row = ac_all_t[al:al + 1, :]
            a_last = ac_all[CHUNK - 1:CHUNK, al:al + 1]
            xh = x[:, h * SSD_P:(h + 1) * SSD_P]
            xdt = xh * dt
            lmat = jnp.exp(jnp.where(causal, ac - ac_row, NEG))
            hprev = h_old[:, h * SSD_P:(h + 1) * SSD_P]
            y = _dot(cb * lmat, xdt) + _dot(cg, hprev) * jnp.exp(ac) + d_ref[h] * xh
            ys.append(y)
            xdec = xdt * jnp.exp(a_last - ac)
            hs.append(hprev * jnp.exp(a_last) + _dot_tn(bg, xdec))
    h_ref[...] = jnp.concatenate(hs, axis=-1)
    y = jnp.concatenate(ys, axis=-1) * _silu(z_ref[...])
    nw = nw_ref[...]
    gwid = d_ssm // SSD_G
    outs = [_rms(y[:, g * gwid:(g + 1) * gwid], nw[:, g * gwid:(g + 1) * gwid]) for g in range(SSD_G)]
    o_ref[...] = jnp.where(valid, jnp.concatenate(outs, axis=-1), 0.0).astype(o_ref.dtype)


def _ssd(xbc, z, sm, cw, cb, alog_row, dtb_row, d_skip, nw, bsz, nx):
    rows, cwid = xbc.shape
    d_ssm = SSD_H * SSD_P
    rb = lambda b, c: (_chunk_block(b, c, nx), 0)
    fixed = lambda b, c: (0, 0)
    return pl.pallas_call(
        functools.partial(_ssd_kernel, nx=nx),
        grid=(bsz, nx + 2),
        in_specs=[pl.BlockSpec(memory_space=pltpu.SMEM),
                  pl.BlockSpec((CHUNK, cwid), rb),
                  pl.BlockSpec((CHUNK, d_ssm), rb),
                  pl.BlockSpec((CHUNK, CHUNK), rb),
                  pl.BlockSpec((CONV_K, cwid), fixed),
                  pl.BlockSpec((1, cwid), fixed),
                  pl.BlockSpec((1, CHUNK), fixed),
                  pl.BlockSpec((1, CHUNK), fixed),
                  pl.BlockSpec((1, d_ssm), fixed)],
        out_specs=pl.BlockSpec((CHUNK, d_ssm), rb),
        out_shape=jax.ShapeDtypeStruct((rows, d_ssm), BF16),
        scratch_shapes=[pltpu.VMEM((CHUNK + 8, cwid), F32),
                        pltpu.VMEM((SSD_N, d_ssm), F32)],
        compiler_params=pltpu.CompilerParams(dimension_semantics=("arbitrary", "arbitrary")),
        name="ssd",
    )(d_skip, xbc, z, sm, cw, cb, alog_row, dtb_row, nw)


def _outproj_kernel(h_ref, y0_ref, y1_ref, y2_ref, y3_ref, w_ref, o_ref):
    acc = h_ref[...]
    for i, y_ref in enumerate((y0_ref, y1_ref, y2_ref, y3_ref)):
        wd = y_ref.shape[1]
        acc = acc + jnp.dot(y_ref[...].astype(BF16), w_ref[i * wd:(i + 1) * wd, :],
                            preferred_element_type=F32)
    o_ref[...] = acc


def _outproj(h, ys, w):
    rows, d = h.shape
    tm = TM_OUT
    wd = ys[0].shape[1]
    return pl.pallas_call(
        _outproj_kernel,
        grid=(rows // tm,),
        in_specs=[pl.BlockSpec((tm, d), lambda i: (i, 0))]
                 + [pl.BlockSpec((tm, wd), lambda i: (i, 0)) for _ in ys]
                 + [pl.BlockSpec(w.shape, lambda i: (0, 0))],
        out_specs=pl.BlockSpec((tm, d), lambda i: (i, 0)),
        out_shape=jax.ShapeDtypeStruct((rows, d), F32),
        compiler_params=pltpu.CompilerParams(dimension_semantics=("arbitrary",),
                                             vmem_limit_bytes=VMEM_LIMIT),
        name="outproj",
    )(h, *ys, w)


def _ffn_kernel(h_ref, nw_ref, wg_ref, wu_ref, wd_ref, o_ref, u_scr, acc_scr):
    f = pl.program_id(1)

    @pl.when(f == 0)
    def _():
        u_scr[...] = _rms(h_ref[...], nw_ref[...]).astype(BF16)
        acc_scr[...] = jnp.zeros_like(acc_scr)

    u = u_scr[...]
    g = jnp.dot(u, wg_ref[...], preferred_element_type=F32)
    up = jnp.dot(u, wu_ref[...], preferred_element_type=F32)
    a = (_silu(g) * up).astype(BF16)
    acc_scr[...] += jnp.dot(a, wd_ref[...], preferred_element_type=F32)

    @pl.when(f == pl.num_programs(1) - 1)
    def _():
        o_ref[...] = h_ref[...] + acc_scr[...]


def _ffn(h, nw, wg, wu, wd):
    rows, d = h.shape
    dff = wg.shape[1]
    tm, tf = TM_FFN, TF_DENSE
    return pl.pallas_call(
        _ffn_kernel,
        grid=(rows // tm, dff // tf),
        in_specs=[pl.BlockSpec((tm, d), lambda i, f: (i, 0)),
                  pl.BlockSpec((1, d), lambda i, f: (0, 0)),
                  pl.BlockSpec((d, tf), lambda i, f: (0, f)),
                  pl.BlockSpec((d, tf), lambda i, f: (0, f)),
                  pl.BlockSpec((tf, d), lambda i, f: (f, 0))],
        out_specs=pl.BlockSpec((tm, d), lambda i, f: (i, 0)),
        out_shape=jax.ShapeDtypeStruct((rows, d), F32),
        scratch_shapes=[pltpu.VMEM((tm, d), BF16), pltpu.VMEM((tm, d), F32)],
        compiler_params=pltpu.CompilerParams(dimension_semantics=("arbitrary", "arbitrary"),
                                             vmem_limit_bytes=VMEM_LIMIT),
        name="ffn_dense",
    )(h, nw, wg, wu, wd)


def _router_kernel(h_ref, nw_ref, wr_ref, u_ref, r_ref):
    u = _rms(h_ref[...], nw_ref[...])
    u_ref[...] = u
    lane = _iota((1, CHUNK), 1)
    logits = jnp.where(lane < N_EXP, _dot_hi(u, wr_ref[...]), -jnp.inf)
    m1 = jnp.max(logits, axis=-1, keepdims=True)
    i1 = jnp.min(jnp.where(logits == m1, lane, CHUNK), axis=-1, keepdims=True)
    rest = jnp.where(lane == i1, -jnp.inf, logits)
    m2 = jnp.max(rest, axis=-1, keepdims=True)
    i2 = jnp.min(jnp.where(rest == m2, lane, CHUNK), axis=-1, keepdims=True)
    e = jnp.exp(m2 - m1)
    g1 = 1.0 / (1.0 + e)
    g2 = e / (1.0 + e)
    out = jnp.where(lane == 0, i1.astype(F32), 0.0)
    out = jnp.where(lane == 1, i2.astype(F32), out)
    out = jnp.where(lane == 2, g1, out)
    out = jnp.where(lane == 3, g2, out)
    r_ref[...] = out


def _router(h, nw, wr):
    rows, d = h.shape
    tm = TM_OUT
    return pl.pallas_call(
        _router_kernel,
        grid=(rows // tm,),
        in_specs=[pl.BlockSpec((tm, d), lambda i: (i, 0)),
                  pl.BlockSpec((1, d), lambda i: (0, 0)),
                  pl.BlockSpec((d, CHUNK), lambda i: (0, 0))],
        out_specs=[pl.BlockSpec((tm, d), lambda i: (i, 0)),
                   pl.BlockSpec((tm, CHUNK), lambda i: (i, 0))],
        out_shape=[jax.ShapeDtypeStruct((rows, d), F32),
                   jax.ShapeDtypeStruct((rows, CHUNK), F32)],
        compiler_params=pltpu.CompilerParams(dimension_semantics=("arbitrary",)),
        name="moe_router",
    )(h, nw, wr)


def _expert_kernel(be_ref, nu_ref, tok_ref, u_hbm, wg_ref, wu_ref, wd_ref, o_ref, acc_scr, xraw, xbf, sems,
                   *, nf):
    i = pl.program_id(0)
    f = pl.program_id(1)
    last = f == pl.num_programs(1) - 1
    n_used = nu_ref[0]
    used = i < n_used
    tm = xbf.shape[0]

    def gather_block(j, slot):
        def issue(r, carry):
            t = tok_ref[j * tm + r]
            pltpu.make_async_copy(u_hbm.at[pl.ds(t, 1), :], xraw.at[slot, pl.ds(r, 1), :],
                                  sems.at[slot]).start()
            return carry
        lax.fori_loop(0, tm, issue, 0, unroll=8)

    @pl.when(jnp.logical_and(used, f == 0))
    def _():
        slot = i % 2

        @pl.when(i == 0)
        def _():
            gather_block(0, 0)

        pltpu.make_async_copy(u_hbm.at[pl.ds(0, tm), :], xraw.at[slot], sems.at[slot]).wait()
        xbf[...] = xraw[slot].astype(BF16)
        acc_scr[...] = jnp.zeros_like(acc_scr)

    def swiglu_step():
        x = xbf[...]
        g = jnp.dot(x, wg_ref[0], preferred_element_type=F32)
        up = jnp.dot(x, wu_ref[0], preferred_element_type=F32)
        a = (_silu(g) * up).astype(BF16)
        acc_scr[...] += jnp.dot(a, wd_ref[0], preferred_element_type=F32)

    has_next = i + 1 < n_used

    @pl.when(jnp.logical_and(used, has_next))
    def _():
        per_step = tm // nf
        slot = 1 - i % 2
        base = (i + 1) * tm + f * per_step
        for r in range(per_step):
            t = tok_ref[base + r]
            pltpu.make_async_copy(u_hbm.at[pl.ds(t, 1), :], xraw.at[slot, pl.ds(f * per_step + r, 1), :],
                                  sems.at[slot]).start()
        swiglu_step()

    @pl.when(jnp.logical_and(used, jnp.logical_not(has_next)))
    def _():
        swiglu_step()

    @pl.when(jnp.logical_and(used, last))
    def _():
        o_ref[...] = acc_scr[...]

    @pl.when(jnp.logical_and(jnp.logical_not(used), last))
    def _():
        o_ref[...] = jnp.zeros_like(o_ref)


def _experts(blk_exp, n_used, buf_tok, u, wg, wu, wd):
    cap = buf_tok.shape[0]
    d = u.shape[1]
    dff = wg.shape[2]
    tm, tf = TM_MOE, TF_MOE
    nf = dff // tf
    assert tm % nf == 0
    return pl.pallas_call(
        functools.partial(_expert_kernel, nf=nf),
        grid_spec=pltpu.PrefetchScalarGridSpec(
            num_scalar_prefetch=3,
            grid=(cap // tm, dff // tf),
            in_specs=[pl.BlockSpec(memory_space=pl.ANY),
                      pl.BlockSpec((1, d, tf), lambda i, f, be, nu, tok: (be[i], 0, f)),
                      pl.BlockSpec((1, d, tf), lambda i, f, be, nu, tok: (be[i], 0, f)),
                      pl.BlockSpec((1, tf, d), lambda i, f, be, nu, tok: (be[i], f, 0))],
            out_specs=pl.BlockSpec((tm, d), lambda i, f, be, nu, tok: (i, 0)),
            scratch_shapes=[pltpu.VMEM((tm, d), F32),
                            pltpu.VMEM((2, tm, d), F32),
                            pltpu.VMEM((tm, d), BF16),
                            pltpu.SemaphoreType.DMA((2,))]),
        out_shape=jax.ShapeDtypeStruct((cap, d), F32),
        compiler_params=pltpu.CompilerParams(dimension_semantics=("arbitrary", "arbitrary"),
                                             vmem_limit_bytes=VMEM_LIMIT),
        name="moe_experts",
    )(blk_exp, n_used, buf_tok, u, wg, wu, wd)


def _combine_kernel(d1_ref, d2_ref, h_ref, r_ref, nw_ref, yb_hbm, o_ref, buf1, buf2, sem, *, nx):
    b = pl.program_id(0)
    c = pl.program_id(1)
    base = (b * (nx + 2) + c) * CHUNK

    def issue(r, carry):
        pltpu.make_async_copy(yb_hbm.at[pl.ds(d1_ref[base + r], 1), :], buf1.at[pl.ds(r, 1), :], sem).start()
        pltpu.make_async_copy(yb_hbm.at[pl.ds(d2_ref[base + r], 1), :], buf2.at[pl.ds(r, 1), :], sem).start()
        return carry

    lax.fori_loop(0, CHUNK, issue, 0, unroll=8)
    pltpu.make_async_copy(yb_hbm.at[pl.ds(0, CHUNK), :], buf1, sem).wait()
    pltpu.make_async_copy(yb_hbm.at[pl.ds(0, CHUNK), :], buf2, sem).wait()
    r = r_ref[...]
    y = h_ref[...] + r[:, 2:3] * buf1[...] + r[:, 3:4] * buf2[...]
    o_ref[0] = _rms(y, nw_ref[...])


def _combine_final(d1, d2, h, route, nw, yb, bsz, nx):
    d = h.shape[1]
    seq = nx * CHUNK
    rb = lambda b, c, d1, d2: (b * (nx + 2) + c, 0)
    return pl.pallas_call(
        functools.partial(_combine_kernel, nx=nx),
        grid_spec=pltpu.PrefetchScalarGridSpec(
            num_scalar_prefetch=2,
            grid=(bsz, nx),
            in_specs=[pl.BlockSpec((CHUNK, d), rb),
                      pl.BlockSpec((CHUNK, CHUNK), rb),
                      pl.BlockSpec((1, d), lambda b, c, d1, d2: (0, 0)),
                      pl.BlockSpec(memory_space=pl.ANY)],
            out_specs=pl.BlockSpec((1, CHUNK, d), lambda b, c, d1, d2: (b, c, 0)),
            scratch_shapes=[pltpu.VMEM((CHUNK, d), F32), pltpu.VMEM((CHUNK, d), F32),
                            pltpu.SemaphoreType.DMA(())]),
        out_shape=jax.ShapeDtypeStruct((bsz, seq, d), F32),
        compiler_params=pltpu.CompilerParams(dimension_semantics=("arbitrary", "arbitrary")),
        name="moe_combine_final",
    )(d1, d2, h, route, nw, yb)


def _moe_slots(route, tm, nx):
    rows = route.shape[0]
    n_assign = 2 * rows
    pos = jnp.arange(rows, dtype=jnp.int32) % ((nx + 2) * CHUNK)
    real = jnp.logical_or(pos < nx * CHUNK,
                          jnp.logical_and(pos >= nx * CHUNK + PAD, pos < (nx + 1) * CHUNK))
    e = jnp.where(real[:, None], route[:, :2].astype(jnp.int32), N_EXP)
    flat_e = e.reshape(-1)
    onehot = (flat_e[:, None] == jnp.arange(N_EXP, dtype=jnp.int32)[None, :]).astype(jnp.int32)
    csum = jnp.cumsum(onehot, axis=0)
    rank = jnp.sum(csum * onehot, axis=1) - 1
    counts = csum[-1]
    padded = ((counts + tm - 1) // tm) * tm
    ends_p = jnp.cumsum(padded)
    start_p = ends_p - padded
    n_real = n_assign // ((nx + 2) * CHUNK) * (nx * CHUNK + N_META)
    nblk = -(-n_real // tm) + N_EXP
    cap = nblk * tm
    dest = jnp.where(flat_e < N_EXP, jnp.sum(start_p[None, :] * onehot, axis=1) + rank, cap)
    buf_tok = jnp.zeros((cap,), jnp.int32).at[dest].set(jnp.arange(n_assign, dtype=jnp.int32) // 2,
                                                        mode="drop")
    blk_start = jnp.arange(nblk, dtype=jnp.int32) * tm
    blk_exp = jnp.minimum(jnp.sum((ends_p[None, :] <= blk_start[:, None]).astype(jnp.int32), axis=-1),
                          N_EXP - 1).astype(jnp.int32)
    n_used = (ends_p[-1:] // tm).astype(jnp.int32)
    dest2 = dest.reshape(rows, 2)
    return buf_tok, blk_exp, n_used, dest2[:, 0], dest2[:, 1], cap


def _lane_row(vals, offset):
    return jnp.zeros((1, CHUNK), F32).at[0, offset:offset + vals.shape[0]].set(vals.astype(F32))


def kernel(x, meta_tokens, norm_mix, w_in, dn_conv_w, dn_a_log, dn_dt_bias, dn_norm_w, df_lambda, df_norm_w, sw_sinks, ssd_conv_w, ssd_conv_b, ssd_a_log, ssd_dt_bias, ssd_d, ssd_norm_w, w_out, norm_ffn, ffn_w_gate, ffn_w_up, ffn_w_down, moe_router, moe_w_gate, moe_w_up, moe_w_down, norm_final):
    bsz, seq, d = x.shape
    depth = w_in.shape[0]
    assert seq % TQ == 0 and depth == 2
    nx = seq // CHUNK
    rows = bsz * (nx + 2) * CHUNK

    meta = jnp.broadcast_to(meta_tokens[None].astype(F32), (bsz, N_META, d))
    h = jnp.concatenate([x.astype(F32), jnp.zeros((bsz, PAD, d), F32), meta,
                         jnp.zeros((bsz, CHUNK, d), F32)], axis=1).reshape(rows, d)

    y = None
    for l in range(depth):
        wl = w_in[l]
        w_cat = jnp.concatenate(
            [wl[:, 0:1024], wl[:, 1024:1032], wl[:, 3336:3340], jnp.zeros((d, CHUNK - 12), wl.dtype),
             wl[:, 1032:2312], wl[:, 2312:3336]], axis=1).astype(BF16)
        (dn_qkv, dn_z, small, df_q, df_k, df_v, sw_q, sw_kv, ssd_z, ssd_xbc) = _inproj(
            h, norm_mix[l][None].astype(F32), w_cat)

        y_dn = _deltanet(dn_qkv, dn_z, small, dn_conv_w[l].astype(F32),
                         _lane_row(dn_a_log[l], SM_DN_A), _lane_row(dn_dt_bias[l], SM_DN_A),
                         dn_norm_w[l][None].astype(F32), bsz, nx)
        lambda_init = 0.8 - 0.6 * math.exp(-0.3 * l)
        y_df = _diff_attention(df_q, df_k, df_v, df_lambda[l].astype(F32),
                               df_norm_w[l][:, None].astype(F32), lambda_init, bsz, nx)
        y_sw = _swa(sw_q, sw_kv, sw_sinks[l].astype(F32), bsz, nx)
        y_ssd = _ssd(ssd_xbc, ssd_z, small, ssd_conv_w[l].astype(F32), ssd_conv_b[l][None].astype(F32),
                     _lane_row(ssd_a_log[l], SM_SSD_DT), _lane_row(ssd_dt_bias[l], SM_SSD_DT),
                     ssd_d[l].astype(F32), ssd_norm_w[l][None].astype(F32), bsz, nx)
        h = _outproj(h, (y_dn, y_df, y_sw, y_ssd), w_out[l].astype(BF16))

        i = l // 2
        if l % 2 == 0:
            h = _ffn(h, norm_ffn[l][None].astype(F32), ffn_w_gate[i].astype(BF16),
                     ffn_w_up[i].astype(BF16), ffn_w_down[i].astype(BF16))
        else:
            wr = jnp.zeros((d, CHUNK), F32).at[:, :N_EXP].set(moe_router[i].astype(F32))
            u, route = _router(h, norm_ffn[l][None].astype(F32), wr)
            buf_tok, blk_exp, n_used, d1, d2, cap = _moe_slots(route, TM_MOE, nx)
            yb = _experts(blk_exp, n_used, buf_tok, u, moe_w_gate[i].astype(BF16),
                          moe_w_up[i].astype(BF16), moe_w_down[i].astype(BF16))
            y = _combine_final(d1, d2, h, route, norm_final[None].astype(F32), yb, bsz, nx)
    return y
```
